```python
import math
import jax, jax.numpy as jnp
from jax import lax
import numpy as np

D_MODEL = 1024
BATCH = 4
SEQ = 8192
DEPTH = 1

HEAD_DIM = 64
RWKV_HEADS = 8
RWKV_WIDTH = RWKV_HEADS * HEAD_DIM
ATT_HEADS = 8
ATT_WIDTH = ATT_HEADS * HEAD_DIM
IDX_HEADS = 8
IDX_DIM = 64
DECAY_RANK = 64
AAA_RANK = 64
GATE_RANK = 128
MAX_TOPK = 256
Q_BLOCK = 128
ROPE_THETA = 10000.0
D_FF = ((8 * D_MODEL // 3 + 255) // 256) * 256
NORM_EPS = 1e-6
LNX_EPS = 64e-5

RWKV_COLS = 3 * RWKV_WIDTH + DECAY_RANK + AAA_RANK + GATE_RANK
ATT_COLS = 3 * ATT_WIDTH
IDX_COLS = IDX_HEADS * IDX_DIM + IDX_DIM + IDX_HEADS
GATE_COLS = 2 * D_MODEL
IN_COLS = RWKV_COLS + ATT_COLS + IDX_COLS + GATE_COLS

kernel_name = "hybrid_rwkv7_dsa_gated_block"


def rmsnorm(x, g):
    xf = x.astype(jnp.float32)
    y = xf * lax.rsqrt(jnp.mean(xf * xf, axis=-1, keepdims=True) + NORM_EPS)
    return (y * g.astype(jnp.float32)).astype(x.dtype)


def rope(t):
    S, D = t.shape[1], t.shape[-1]
    half = D // 2
    inv = 1.0 / (ROPE_THETA ** (jnp.arange(half, dtype=jnp.float32) * 2.0 / D))
    ang = jnp.arange(S, dtype=jnp.float32)[:, None] * inv[None, :]
    cos = jnp.cos(ang)[None, :, None, :]
    sin = jnp.sin(ang)[None, :, None, :]
    tf = t.astype(jnp.float32)
    t1, t2 = tf[..., :half], tf[..., half:]
    return jnp.concatenate([t1 * cos - t2 * sin, t1 * sin + t2 * cos], axis=-1).astype(t.dtype)


def token_shift(p, mu):
    prev = jnp.pad(p, ((0, 0), (1, 0), (0, 0)))[:, :-1]
    return p + (prev - p) * mu


def rwkv7_mix(p, w_decay_up, w0, a_up, a0, g_up, k_k, k_a, r_k, lnx_g, lnx_b):
    B, S, _ = p.shape
    H, N = RWKV_HEADS, HEAD_DIM
    c = np.cumsum([RWKV_WIDTH, RWKV_WIDTH, RWKV_WIDTH, DECAY_RANK, AAA_RANK])
    r, k, v, wd, ad, gd = jnp.split(p, c, axis=-1)
    w = -jax.nn.softplus(-(w0 + jnp.tanh(wd) @ w_decay_up)) - 0.5
    a = jax.nn.sigmoid(a0 + ad @ a_up)
    g = jax.nn.sigmoid(gd) @ g_up
    kk = (k * k_k).reshape(B, S, H, N).astype(jnp.float32)
    kk = kk / jnp.maximum(jnp.linalg.norm(kk, axis=-1, keepdims=True), 1e-12)
    k = k * (1.0 + (a - 1.0) * k_a)
    heads = lambda t: t.reshape(B, S, H, N).astype(jnp.float32)
    r_h, k_h, v_h, a_h = heads(r), heads(k), heads(v), heads(a)
    decay = jnp.exp(-jnp.exp(heads(w)))
    tm = lambda t: jnp.moveaxis(t, 1, 0)

    def step(state, inp):
        r_t, d_t, k_t, v_t, kk_t, a_t = inp
        sa = jnp.einsum('bhij,bhj->bhi', state, -kk_t)
        state = (state * d_t[:, :, None, :]
                 + sa[..., None] * (kk_t * a_t)[:, :, None, :]
                 + v_t[..., None] * k_t[:, :, None, :])
        y_t = jnp.einsum('bhij,bhj->bhi', state, r_t)
        return state, y_t

    state0 = jnp.zeros((B, H, N, N), jnp.float32)
    _, y = lax.scan(step, state0, (tm(r_h), tm(decay), tm(k_h), tm(v_h), tm(kk), tm(a_h)))
    y = jnp.moveaxis(y, 0, 1)
    mean = jnp.mean(y, axis=-1, keepdims=True)
    var = jnp.mean(jnp.square(y - mean), axis=-1, keepdims=True)
    y = (y - mean) * lax.rsqrt(var + LNX_EPS)
    y = y * lnx_g.reshape(H, N) + lnx_b.reshape(H, N)
    bonus = jnp.sum(r_h * k_h * r_k.astype(jnp.float32), axis=-1, keepdims=True) * v_h
    y = (y + bonus).reshape(B, S, RWKV_WIDTH).astype(p.dtype)
    return y * g


def dsa_attention(q, k, v, q_idx, k_idx, w_idx):
    B, S, H, Dh = q.shape
    topk = min(MAX_TOPK, S // 4)
    nb = S // Q_BLOCK
    blk = lambda t: jnp.moveaxis(t.reshape((B, nb, Q_BLOCK) + t.shape[2:]), 1, 0)
    key_pos = jnp.arange(S)
    idx_scale = IDX_DIM ** -0.5 * IDX_HEADS ** -0.5

    def one_block(args):
        qb, qib, wib, bi = args
        t = bi * Q_BLOCK + jnp.arange(Q_BLOCK)
        sc = jax.nn.relu(jnp.einsum('bqhd,bsd->bqhs', qib, k_idx).astype(jnp.float32))
        score = jnp.einsum('bqhs,bqh->bqs', sc, wib.astype(jnp.float32)) * idx_scale
        causal = key_pos[None, None, :] <= t[None, :, None]
        score = jnp.where(causal, score, -jnp.inf)
        _, sel = lax.top_k(score, topk)
        valid = sel <= t[None, :, None]
        kg = jax.vmap(lambda kb, ib: kb[ib])(k, sel)
        vg = jax.vmap(lambda vb, ib: vb[ib])(v, sel)
        logits = jnp.einsum('bqhd,bqkhd->bqhk', qb, kg).astype(jnp.float32) * Dh ** -0.5
        logits = jnp.where(valid[:, :, None, :], logits, -jnp.inf)
        prob = jax.nn.softmax(logits, axis=-1).astype(vg.dtype)
        return jnp.einsum('bqhk,bqkhd->bqhd', prob, vg)

    out = lax.map(one_block, (blk(q), blk(q_idx), blk(w_idx), jnp.arange(nb)))
    return jnp.moveaxis(out, 0, 1).reshape(B, S, H * Dh)


def setup_inputs(seed: int = 0) -> dict:
    key = jax.random.key(seed)
    ks = jax.random.split(key, 24)
    nrm = lambda k, shape, s: jax.random.normal(k, shape, jnp.float32) * s
    return {
        "x": jax.random.normal(ks[0], (BATCH, SEQ, D_MODEL), jnp.float32),
        "norm1_g": 1.0 + nrm(ks[1], (D_MODEL,), 0.05),
        "w_in": nrm(ks[2], (D_MODEL, IN_COLS), D_MODEL ** -0.5),
        "tshift_mu": jax.random.uniform(ks[3], (RWKV_COLS,), jnp.float32),
        "w_decay_up": nrm(ks[4], (DECAY_RANK, RWKV_WIDTH), 0.5 * DECAY_RANK ** -0.5),
        "w0": nrm(ks[5], (RWKV_WIDTH,), 0.5),
        "a_up": nrm(ks[6], (AAA_RANK, RWKV_WIDTH), 0.5 * AAA_RANK ** -0.5),
        "a0": nrm(ks[7], (RWKV_WIDTH,), 0.5),
        "g_up": nrm(ks[8], (GATE_RANK, RWKV_WIDTH), GATE_RANK ** -0.5),
        "k_k": 0.85 + nrm(ks[9], (RWKV_WIDTH,), 0.05),
        "k_a": 1.0 + nrm(ks[10], (RWKV_WIDTH,), 0.05),
        "r_k": nrm(ks[11], (RWKV_HEADS, HEAD_DIM), 0.1),
        "lnx_g": 1.0 + nrm(ks[12], (RWKV_WIDTH,), 0.05),
        "lnx_b": nrm(ks[13], (RWKV_WIDTH,), 0.01),
        "w_o_rwkv": nrm(ks[14], (RWKV_WIDTH, D_MODEL), RWKV_WIDTH ** -0.5),
        "w_o_att": nrm(ks[15], (ATT_WIDTH, D_MODEL), ATT_WIDTH ** -0.5),
        "w_out": nrm(ks[16], (D_MODEL, D_MODEL), D_MODEL ** -0.5),
        "norm2_g": 1.0 + nrm(ks[17], (D_MODEL,), 0.05),
        "w_ffn_in": nrm(ks[18], (D_MODEL, 2 * D_FF), D_MODEL ** -0.5),
        "w_ffn_out": nrm(ks[19], (D_FF, D_MODEL), D_FF ** -0.5),
        "normf_g": 1.0 + nrm(ks[20], (D_MODEL,), 0.05),
    }


def reference(x, norm1_g, w_in, tshift_mu, w_decay_up, w0, a_up, a0, g_up, k_k, k_a, r_k,
              lnx_g, lnx_b, w_o_rwkv, w_o_att, w_out, norm2_g, w_ffn_in, w_ffn_out, normf_g):
    B, S, _ = x.shape
    h = x
    for _layer in range(DEPTH):
        u = rmsnorm(h, norm1_g)
        proj = u @ w_in
        c = np.cumsum([RWKV_COLS, ATT_COLS, IDX_COLS])
        p_rwkv, p_att, p_idx, p_gate = jnp.split(proj, c, axis=-1)

        y_a = rwkv7_mix(token_shift(p_rwkv, tshift_mu), w_decay_up, w0, a_up, a0, g_up,
                        k_k, k_a, r_k, lnx_g, lnx_b)

        q, k, v = jnp.split(p_att, 3, axis=-1)
        hv = lambda t: t.reshape(B, S, ATT_HEADS, HEAD_DIM)
        q, k, v = rope(hv(q)), rope(hv(k)), hv(v)
        ci = np.cumsum([IDX_HEADS * IDX_DIM, IDX_DIM])
        q_idx, k_idx, w_idx = jnp.split(p_idx, ci, axis=-1)
        q_idx = rope(q_idx.reshape(B, S, IDX_HEADS, IDX_DIM))
        k_idx = rope(k_idx[:, :, None, :])[:, :, 0, :]
        y_b = dsa_attention(q, k, v, q_idx, k_idx, w_idx)

        g_a, g_b = jnp.split(jax.nn.sigmoid(p_gate), 2, axis=-1)
        merged = g_a * (y_a @ w_o_rwkv) + g_b * (y_b @ w_o_att)
        h = h + merged @ w_out

        z = rmsnorm(h, norm2_g) @ w_ffn_in
        z_gate, z_up = jnp.split(z, 2, axis=-1)
        h = h + (jax.nn.silu(z_gate) * z_up) @ w_ffn_out
    return rmsnorm(h, normf_g)
```

```python
import functools
import math

import jax
import jax.numpy as jnp
import numpy as np
from jax import lax
from jax.experimental import pallas as pl
from jax.experimental.pallas import tpu as pltpu

F32 = jnp.float32
BF16 = jnp.bfloat16
I32 = jnp.int32

D_MODEL = 1024
HEAD_DIM = 64
N_HEADS = 8
WIDTH = N_HEADS * HEAD_DIM
DECAY_RANK = 64
AAA_RANK = 64
GATE_RANK = 128
MAX_TOPK = 256
ROPE_THETA = 10000.0
D_FF = 2816
NORM_EPS = 1e-6
LNX_EPS = 64e-5
RWKV_COLS = 3 * WIDTH + DECAY_RANK + AAA_RANK + GATE_RANK
ATT_COLS = 3 * WIDTH
IDX_COLS = N_HEADS * HEAD_DIM + HEAD_DIM + N_HEADS
IDX_PAD = 640
GATE_COLS = 2 * D_MODEL
LANES = 128
VMEM_LIMIT = 56 * 1024 * 1024

INT_MIN = -(2 ** 31)
NEG_BIG = -1e30


def _dot(a, b, precision=None):
    return jnp.dot(a, b, preferred_element_type=F32, precision=precision)


def _dot_nt(a, b, precision=None):
    return lax.dot_general(a, b, (((1,), (1,)), ((), ())), preferred_element_type=F32,
                           precision=precision)


def _dot_tn(a, b, precision=None):
    return lax.dot_general(a, b, (((0,), (0,)), ((), ())), preferred_element_type=F32,
                           precision=precision)


def _sigmoid(x):
    return 1.0 / (1.0 + jnp.exp(-x))


def _rope_group(xg, cos, sin_signed, lane_lo):
    partner = jnp.where(lane_lo, pltpu.roll(xg, LANES - HEAD_DIM // 2, 1),
                        pltpu.roll(xg, HEAD_DIM // 2, 1))
    return xg * cos + partner * sin_signed


def _proj_kernel(x_ref, g1_ref, wr_ref, wa_ref, wi_ref, wg_ref, mu_ref, cos_ref, sin_ref,
                 ps_ref, q_ref, k_ref, v_ref, qi_ref, ki_ref, kw_ref, gate_ref, carry_ref,
                 *, tiles_per_seq):
    i = pl.program_id(0)
    x = x_ref[...]
    tm = x.shape[0]
    u = x * lax.rsqrt(jnp.mean(x * x, axis=-1, keepdims=True) + NORM_EPS) * g1_ref[...]
    ub = u.astype(BF16)

    pr = _dot(ub, wr_ref[...])

    @pl.when(i % tiles_per_seq == 0)
    def _():
        carry_ref[...] = jnp.zeros_like(carry_ref)

    row = lax.broadcasted_iota(I32, (tm, 1), 0)
    prev = jnp.where(row == 0, carry_ref[7:8, :], pltpu.roll(pr, 1, 0))
    ps_ref[...] = pr + (prev - pr) * mu_ref[...]
    carry_ref[...] = pr[tm - 8:tm, :]

    cos = cos_ref[...]
    sin_signed = sin_ref[...]
    lane = lax.broadcasted_iota(I32, (1, LANES), 1)
    lane_lo = (lane % HEAD_DIM) < HEAD_DIM // 2

    pa = _dot(ub, wa_ref[...])
    for g in range(WIDTH // LANES):
        sl = slice(g * LANES, (g + 1) * LANES)
        qg = _rope_group(pa[:, sl], cos, sin_signed, lane_lo) * (HEAD_DIM ** -0.5)
        q_ref[:, sl] = qg.astype(BF16)
        kg = _rope_group(pa[:, WIDTH + g * LANES:WIDTH + (g + 1) * LANES], cos, sin_signed, lane_lo)
        k_ref[:, sl] = kg.astype(BF16)
    v_ref[...] = pa[:, 2 * WIDTH:].astype(BF16)

    pi = _dot(ub, wi_ref[...])
    for g in range(WIDTH // LANES):
        sl = slice(g * LANES, (g + 1) * LANES)
        qi_ref[:, sl] = _rope_group(pi[:, sl], cos, sin_signed, lane_lo).astype(BF16)
    tail = pi[:, WIDTH:WIDTH + LANES]
    kw_ref[...] = tail
    ki_ref[...] = _rope_group(tail, cos, sin_signed, lane_lo)[:, :HEAD_DIM].astype(BF16)

    gate_ref[...] = _sigmoid(_dot(ub, wg_ref[...]))


def _rope_tables(seq):
    half = HEAD_DIM // 2
    inv = 1.0 / (ROPE_THETA ** (jnp.arange(half, dtype=F32) * 2.0 / HEAD_DIM))
    ang = jnp.arange(seq, dtype=F32)[:, None] * inv[None, :]
    cos, sin = jnp.cos(ang), jnp.sin(ang)
    cos_t = jnp.concatenate([cos, cos, cos, cos], axis=1)
    sin_t = jnp.concatenate([-sin, sin, -sin, sin], axis=1)
    return cos_t, sin_t


def _full(shape):
    return pl.BlockSpec(shape, lambda *_: (0,) * len(shape))


def _project(x2, norm1_g, w_in, tshift_mu, seq):
    tokens = x2.shape[0]
    tm = 256 if seq % 256 == 0 else seq
    c0, c1, c2 = RWKV_COLS, RWKV_COLS + ATT_COLS, RWKV_COLS + ATT_COLS + IDX_COLS
    wr = w_in[:, :c0].astype(BF16)
    wa = w_in[:, c0:c1].astype(BF16)
    wi = jnp.pad(w_in[:, c1:c2], ((0, 0), (0, IDX_PAD - IDX_COLS))).astype(BF16)
    wg = w_in[:, c2:].astype(BF16)
    cos_t, sin_t = _rope_tables(seq)
    tps = seq // tm
    row = lambda w: pl.BlockSpec((tm, w), lambda i: (i, 0))
    pos = pl.BlockSpec((tm, LANES), lambda i: (i % tps, 0))
    out_shape = (
        jax.ShapeDtypeStruct((tokens, RWKV_COLS), F32),
        jax.ShapeDtypeStruct((tokens, WIDTH), BF16),
        jax.ShapeDtypeStruct((tokens, WIDTH), BF16),
        jax.ShapeDtypeStruct((tokens, WIDTH), BF16),
        jax.ShapeDtypeStruct((tokens, WIDTH), BF16),
        jax.ShapeDtypeStruct((tokens, HEAD_DIM), BF16),
        jax.ShapeDtypeStruct((tokens, LANES), F32),
        jax.ShapeDtypeStruct((tokens, GATE_COLS), F32),
    )
    return pl.pallas_call(
        functools.partial(_proj_kernel, tiles_per_seq=tps),
        grid=(tokens // tm,),
        in_specs=[row(D_MODEL), _full((1, D_MODEL)), _full((D_MODEL, RWKV_COLS)),
                  _full((D_MODEL, ATT_COLS)), _full((D_MODEL, IDX_PAD)), _full((D_MODEL, GATE_COLS)),
                  _full((1, RWKV_COLS)), pos, pos],
        out_specs=(row(RWKV_COLS), row(WIDTH), row(WIDTH), row(WIDTH), row(WIDTH),
                   row(HEAD_DIM), row(LANES), row(GATE_COLS)),
        out_shape=out_shape,
        scratch_shapes=[pltpu.VMEM((8, RWKV_COLS), F32)],
        compiler_params=pltpu.CompilerParams(dimension_semantics=("arbitrary",),
                                             vmem_limit_bytes=VMEM_LIMIT),
        name="proj",
    )(x2, norm1_g.reshape(1, -1), wr, wa, wi, wg, tshift_mu.reshape(1, -1), cos_t, sin_t)


RWKV_CHUNK = 64
HI = lax.Precision.HIGHEST


def _pair_sum(x, lane_first):
    s0 = jnp.sum(jnp.where(lane_first, x, 0.0), axis=1, keepdims=True)
    s1 = jnp.sum(jnp.where(lane_first, 0.0, x), axis=1, keepdims=True)
    return jnp.where(lane_first, s0, s1)


def _rwkv_kernel(ps_ref, wdu_ref, w0_ref, aup_ref, a0_ref, gup_ref, kk_ref, ka_ref, rk_ref,
                 lng_ref, lnb_ref, y_ref, state_ref):
    c = pl.program_id(1)
    L = RWKV_CHUNK

    @pl.when(c == 0)
    def _():
        state_ref[...] = jnp.zeros_like(state_ref)

    ps = ps_ref[...]
    r_all = ps[:, :WIDTH]
    k_all = ps[:, WIDTH:2 * WIDTH]
    v_all = ps[:, 2 * WIDTH:3 * WIDTH]
    o = 3 * WIDTH
    wd = ps[:, o:o + DECAY_RANK]
    ad = ps[:, o + DECAY_RANK:o + DECAY_RANK + AAA_RANK]
    gd = ps[:, o + DECAY_RANK + AAA_RANK:]

    z = -(w0_ref[...] + _dot(jnp.tanh(wd), wdu_ref[...], HI))
    softplus = jnp.maximum(z, 0.0) + jnp.log(1.0 + jnp.exp(-jnp.abs(z)))
    w_all = -softplus - 0.5
    ld_all = -jnp.exp(w_all)
    a_all = _sigmoid(a0_ref[...] + _dot(ad, aup_ref[...], HI))
    g_all = _dot(_sigmoid(gd), gup_ref[...], HI)

    lane = lax.broadcasted_iota(I32, (1, LANES), 1)
    lane_first = lane < HEAD_DIM
    ti = lax.broadcasted_iota(I32, (L, L), 0)
    tj = lax.broadcasted_iota(I32, (L, L), 1)
    strict = ti > tj
    incl = ti >= tj
    bi = lax.broadcasted_iota(I32, (LANES, LANES), 0)
    bj = lax.broadcasted_iota(I32, (LANES, LANES), 1)
    same_head = (bi < HEAD_DIM) == (bj < HEAD_DIM)
    eye = bi == bj
    tri_ones = jnp.where(incl, 1.0, 0.0)

    for g in range(WIDTH // LANES):
        sl = slice(g * LANES, (g + 1) * LANES)
        r, k, v = r_all[:, sl], k_all[:, sl], v_all[:, sl]
        a, ld = a_all[:, sl], ld_all[:, sl]
        kk = k * kk_ref[:, sl]
        nrm = jnp.sqrt(_pair_sum(kk * kk, lane_first))
        kk = kk / jnp.maximum(nrm, 1e-12)
        k = k * (1.0 + (a - 1.0) * ka_ref[:, sl])
        cum = _dot(tri_ones, ld, HI)
        p_inc = jnp.exp(cum)
        p_inv = jnp.exp(-cum)
        al = -kk * jnp.exp(cum - ld)
        be = kk * a * p_inv
        kt = k * p_inv
        rt = r * p_inc
        p_last = p_inc[L - 1:L, :]

        ua_halves, uv_halves, ry_halves, yl_halves = [], [], [], []
        for hh in range(2):
            mask = lane_first if hh == 0 else jnp.logical_not(lane_first)
            alm = jnp.where(mask, al, 0.0)
            rtm = jnp.where(mask, rt, 0.0)
            a_ab = jnp.where(strict, _dot_nt(alm, be, HI), 0.0)
            a_ak = jnp.where(strict, _dot_nt(alm, kt, HI), 0.0)
            a_rb = jnp.where(incl, _dot_nt(rtm, be, HI), 0.0)
            a_rk = jnp.where(incl, _dot_nt(rtm, kt, HI), 0.0)
            rhs = jnp.concatenate([al, _dot(a_ak, v, HI)], axis=1)
            npow = a_ab
            u = rhs + _dot(npow, rhs, HI)
            for _ in range(int(math.log2(L)) - 1):
                npow = _dot(npow, npow, HI)
                u = u + _dot(npow, u, HI)
            ua, uv = u[:, :LANES], u[:, LANES:]
            ua_halves.append(ua)
            uv_halves.append(uv)
            ry_halves.append(rt + _dot(a_rb, ua, HI))
            yl_halves.append(_dot(a_rb, uv, HI) + _dot(a_rk, v, HI))
        ua = jnp.where(lane_first, ua_halves[0], ua_halves[1])
        uv = jnp.where(lane_first, uv_halves[0], uv_halves[1])
        ry = jnp.where(lane_first, ry_halves[0], ry_halves[1])
        yl = jnp.where(lane_first, yl_halves[0], yl_halves[1])

        bep = be * p_last
        ktp = kt * p_last
        m_mat = jnp.where(eye, p_last, 0.0) + jnp.where(same_head, _dot_tn(bep, ua, HI), 0.0)
        g_mat = jnp.where(same_head, _dot_tn(bep, uv, HI) + _dot_tn(ktp, v, HI), 0.0)

        st = state_ref[g]
        y = _dot(ry, st, HI) + yl
        state_ref[g] = _dot(m_mat, st, HI) + g_mat

        mean = _pair_sum(y, lane_first) * (1.0 / HEAD_DIM)
        yc = y - mean
        var = _pair_sum(yc * yc, lane_first) * (1.0 / HEAD_DIM)
        yn = yc * lax.rsqrt(var + LNX_EPS) * lng_ref[:, sl] + lnb_ref[:, sl]
        bonus = _pair_sum(r * k * rk_ref[:, sl], lane_first) * v
        y_ref[:, sl] = (yn + bonus) * g_all[:, sl]


def _rwkv(ps, w_decay_up, w0, a_up, a0, g_up, k_k, k_a, r_k, lnx_g, lnx_b, batch, seq):
    L = RWKV_CHUNK
    n_chunks = seq // L
    vec = lambda t: t.reshape(1, WIDTH)
    return pl.pallas_call(
        _rwkv_kernel,
        grid=(batch, n_chunks),
        in_specs=[pl.BlockSpec((L, RWKV_COLS), lambda b, c: (b * n_chunks + c, 0)),
                  _full((DECAY_RANK, WIDTH)), _full((1, WIDTH)), _full((AAA_RANK, WIDTH)),
                  _full((1, WIDTH)), _full((GATE_RANK, WIDTH)), _full((1, WIDTH)),
                  _full((1, WIDTH)), _full((1, WIDTH)), _full((1, WIDTH)), _full((1, WIDTH))],
        out_specs=pl.BlockSpec((L, WIDTH), lambda b, c: (b * n_chunks + c, 0)),
        out_shape=jax.ShapeDtypeStruct((batch * seq, WIDTH), F32),
        scratch_shapes=[pltpu.VMEM((WIDTH // LANES, LANES, LANES), F32)],
        compiler_params=pltpu.CompilerParams(dimension_semantics=("arbitrary", "arbitrary"),
                                             vmem_limit_bytes=VMEM_LIMIT),
        name="rwkv",
    )(ps, w_decay_up, vec(w0), a_up, vec(a0), g_up, vec(k_k), vec(k_a), vec(r_k),
      vec(lnx_g), vec(lnx_b))


DSA_TQ = 128
DSA_KC = 512


def _dsa_kernel(qi_ref, kw_ref, q_ref, ki_ref, k_ref, v_ref, o_ref,
                keys_ref, m_ref, l_ref, acc_ref, *, seq, topk, kc):
    tq = DSA_TQ
    qb = pl.program_id(1)
    n_kc = (qb * tq + tq + kc - 1) // kc
    t_row = qb * tq + lax.broadcasted_iota(I32, (tq, 1), 0)
    col0 = lax.broadcasted_iota(I32, (1, kc), 1)
    idx_scale = (HEAD_DIM ** -0.5) * (N_HEADS ** -0.5)

    qi = qi_ref[...]
    kw = kw_ref[...]
    qi_h = [qi[:, h * HEAD_DIM:(h + 1) * HEAD_DIM] for h in range(N_HEADS)]
    w_h = [kw[:, HEAD_DIM + h:HEAD_DIM + h + 1] * idx_scale for h in range(N_HEADS)]

    def score_chunk(c, carry):
        off = pl.multiple_of(c * kc, kc)
        kic = ki_ref[pl.ds(off, kc), :]
        acc = jnp.zeros((tq, kc), F32)
        for h in range(N_HEADS):
            acc = acc + jnp.maximum(_dot_nt(qi_h[h], kic), 0.0) * w_h[h]
        bits = pltpu.bitcast(acc, I32)
        key = jnp.where(bits < 0, bits ^ jnp.int32(0x7FFFFFFF), bits)
        key = jnp.where(acc == 0.0, 0, key)
        key = jnp.where(off + col0 <= t_row, key, INT_MIN)
        keys_ref[:, pl.ds(off, kc)] = key
        return carry

    lax.fori_loop(0, n_kc, score_chunk, 0)

    n_blk = n_kc * (kc // LANES)

    def count(pred):
        def body(j, cnt):
            blk = keys_ref[:, pl.ds(pl.multiple_of(j * LANES, LANES), LANES)]
            return cnt + jnp.where(pred(blk, j), 1.0, 0.0)
        cnt = lax.fori_loop(0, n_blk, body, jnp.zeros((tq, LANES), F32))
        return jnp.sum(cnt, axis=1, keepdims=True)

    kf = float(topk)
    thr = jnp.where(count(lambda blk, j: blk >= 0) >= kf, 0, INT_MIN).astype(I32)

    def bisect(i, thr):
        cand = thr | (jnp.int32(1) << (30 - i))
        return jnp.where(count(lambda blk, j: blk >= cand) >= kf, cand, thr)

    thr = lax.fori_loop(0, 31, bisect, thr)

    n_gt = count(lambda blk, j: blk > thr)
    n_ge = count(lambda blk, j: blk >= thr)
    need = kf - n_gt
    lane = lax.broadcasted_iota(I32, (1, LANES), 1)
    idx_bits = max(1, int(math.ceil(math.log2(seq))))

    def tie_limit():
        def step(i, lim):
            cand = lim | (jnp.int32(1) << (idx_bits - 1 - i))
            below = count(lambda blk, j: (blk == thr) & (j * LANES + lane < cand))
            return jnp.where(below < need, cand, lim)
        return lax.fori_loop(0, idx_bits, step, jnp.zeros((tq, 1), I32))

    has_excess = jnp.max(n_ge - kf) > 0.0
    lim = lax.cond(has_excess, tie_limit, lambda: jnp.full((tq, 1), seq, I32))

    m_ref[...] = jnp.full_like(m_ref, NEG_BIG)
    l_ref[...] = jnp.zeros_like(l_ref)
    acc_ref[...] = jnp.zeros_like(acc_ref)
    q = q_ref[...]
    lane_first = lane < HEAD_DIM
    zero = jnp.zeros((), BF16)
    q_h = []
    for h in range(N_HEADS):
        qg = q[:, (h // 2) * LANES:(h // 2 + 1) * LANES]
        q_h.append(jnp.where(lane_first if h % 2 == 0 else jnp.logical_not(lane_first), qg, zero))

    def attend(c, carry):
        off = pl.multiple_of(c * kc, kc)
        key = keys_ref[:, pl.ds(off, kc)]
        col = off + col0
        tie_ok = jnp.where(col <= lim, 0.0, NEG_BIG)
        bias = jnp.where(key > thr, 0.0, jnp.where(key == thr, tie_ok, NEG_BIG))
        bias = jnp.where(col <= t_row, bias, NEG_BIG)
        for h in range(N_HEADS):
            g = h // 2
            kg = k_ref[pl.ds(off, kc), g * LANES:(g + 1) * LANES]
            vg = v_ref[pl.ds(off, kc), g * LANES:(g + 1) * LANES]
            s = _dot_nt(q_h[h], kg) + bias
            m_old = m_ref[h]
            m_new = jnp.maximum(m_old, jnp.max(s, axis=1, keepdims=True))
            alpha = jnp.exp(m_old - m_new)
            p = jnp.exp(s - m_new)
            l_ref[h] = alpha * l_ref[h] + jnp.sum(p, axis=1, keepdims=True)
            acc_ref[h] = alpha * acc_ref[h] + _dot(p.astype(BF16), vg)
            m_ref[h] = m_new
        return carry

    lax.fori_loop(0, n_kc, attend, 0)

    for g in range(WIDTH // LANES):
        o0 = acc_ref[2 * g] / l_ref[2 * g]
        o1 = acc_ref[2 * g + 1] / l_ref[2 * g + 1]
        o_ref[:, g * LANES:(g + 1) * LANES] = jnp.where(lane_first, o0, o1)


def _dsa(qi, kw, q, ki, k, v, batch, seq):
    tq = DSA_TQ
    kc = DSA_KC if seq % DSA_KC == 0 else tq
    nqb = seq // tq
    topk = min(MAX_TOPK, seq // 4)
    qrow = lambda w: pl.BlockSpec((tq, w), lambda b, i: (b * nqb + i, 0))
    kvrow = lambda w: pl.BlockSpec((seq, w), lambda b, i: (b, 0))
    return pl.pallas_call(
        functools.partial(_dsa_kernel, seq=seq, topk=topk, kc=kc),
        grid=(batch, nqb),
        in_specs=[qrow(WIDTH), qrow(LANES), qrow(WIDTH), kvrow(HEAD_DIM), kvrow(WIDTH), kvrow(WIDTH)],
        out_specs=qrow(WIDTH),
        out_shape=jax.ShapeDtypeStruct((batch * seq, WIDTH), F32),
        scratch_shapes=[pltpu.VMEM((tq, seq), I32),
                        pltpu.VMEM((N_HEADS, tq, 1), F32),
                        pltpu.VMEM((N_HEADS, tq, 1), F32),
                        pltpu.VMEM((N_HEADS, tq, LANES), F32)],
        compiler_params=pltpu.CompilerParams(dimension_semantics=("arbitrary", "arbitrary"),
                                             vmem_limit_bytes=VMEM_LIMIT),
        name="dsa",
    )(qi, kw, q, ki, k, v)


def _tail_kernel(x_ref, ya_ref, yb_ref, gate_ref, wor_ref, woa_ref, wout_ref, g2_ref,
                 wfg_ref, wfu_ref, wfo_ref, gf_ref, o_ref):
    gate = gate_ref[...]
    ma = _dot(ya_ref[...].astype(BF16), wor_ref[...])
    mb = _dot(yb_ref[...].astype(BF16), woa_ref[...])
    merged = gate[:, :D_MODEL] * ma + gate[:, D_MODEL:] * mb
    h = x_ref[...] + _dot(merged.astype(BF16), wout_ref[...])
    hn = h * lax.rsqrt(jnp.mean(h * h, axis=-1, keepdims=True) + NORM_EPS) * g2_ref[...]
    hb = hn.astype(BF16)
    ffn = jnp.zeros_like(h)
    half = D_FF // 2
    for f in range(2):
        sl = slice(f * half, (f + 1) * half)
        zg = _dot(hb, wfg_ref[:, sl])
        zu = _dot(hb, wfu_ref[:, sl])
        act = zg * _sigmoid(zg) * zu
        ffn = ffn + _dot(act.astype(BF16), wfo_ref[sl, :])
    h = h + ffn
    o_ref[...] = h * lax.rsqrt(jnp.mean(h * h, axis=-1, keepdims=True) + NORM_EPS) * gf_ref[...]


def _tail(x2, ya, yb, gate, w_o_rwkv, w_o_att, w_out, norm2_g, w_ffn_in, w_ffn_out, normf_g):
    tokens = x2.shape[0]
    tm = 256 if tokens % 256 == 0 else tokens
    row = lambda w: pl.BlockSpec((tm, w), lambda i: (i, 0))
    const = lambda shape: pl.BlockSpec(shape, lambda i: (0, 0), pipeline_mode=pl.Buffered(1))
    return pl.pallas_call(
        _tail_kernel,
        grid=(tokens // tm,),
        in_specs=[row(D_MODEL), row(WIDTH), row(WIDTH), row(GATE_COLS),
                  const((WIDTH, D_MODEL)), const((WIDTH, D_MODEL)), const((D_MODEL, D_MODEL)),
                  const((1, D_MODEL)), const((D_MODEL, D_FF)), const((D_MODEL, D_FF)),
                  const((D_FF, D_MODEL)), const((1, D_MODEL))],
        out_specs=row(D_MODEL),
        out_shape=jax.ShapeDtypeStruct((tokens, D_MODEL), F32),
        compiler_params=pltpu.CompilerParams(dimension_semantics=("arbitrary",),
                                             vmem_limit_bytes=VMEM_LIMIT),
        name="tail",
    )(x2, ya, yb, gate, w_o_rwkv.astype(BF16), w_o_att.astype(BF16), w_out.astype(BF16),
      norm2_g.reshape(1, -1), w_ffn_in[:, :D_FF].astype(BF16), w_ffn_in[:, D_FF:].astype(BF16),
      w_ffn_out.astype(BF16), normf_g.reshape(1, -1))


def kernel(x, norm1_g, w_in, tshift_mu, w_decay_up, w0, a_up, a0, g_up, k_k, k_a, r_k, lnx_g, lnx_b, w_o_rwkv, w_o_att, w_out, norm2_g, w_ffn_in, w_ffn_out, normf_g):
    batch, seq, _ = x.shape
    x2 = x.reshape(batch * seq, D_MODEL)
    ps, q, k, v, qi, ki, kw, gate = _project(x2, norm1_g, w_in, tshift_mu, seq)
    ya = _rwkv(ps, w_decay_up, w0, a_up, a0, g_up, k_k, k_a, r_k.reshape(-1), lnx_g, lnx_b,
               batch, seq)
    yb = _dsa(qi, kw, q, ki, k, v, batch, seq)
    out = _tail(x2, ya, yb, gate, w_o_rwkv, w_o_att, w_out, norm2_g, w_ffn_in, w_ffn_out, normf_g)
    return out.reshape(batch, seq, D_MODEL)
```

```python
import functools
import math

import jax
import jax.numpy as jnp
import numpy as np
from jax import lax
from jax.experimental import pallas as pl
from jax.experimental.pallas import tpu as pltpu

F32 = jnp.float32
BF16 = jnp.bfloat16
I32 = jnp.int32

D_MODEL = 1024
HEAD_DIM = 64
N_HEADS = 8
WIDTH = N_HEADS * HEAD_DIM
DECAY_RANK = 64
AAA_RANK = 64
GATE_RANK = 128
MAX_TOPK = 256
ROPE_THETA = 10000.0
D_FF = 2816
NORM_EPS = 1e-6
LNX_EPS = 64e-5
RWKV_COLS = 3 * WIDTH + DECAY_RANK + AAA_RANK + GATE_RANK
ATT_COLS = 3 * WIDTH
IDX_COLS = N_HEADS * HEAD_DIM + HEAD_DIM + N_HEADS
IDX_PAD = 640
GATE_COLS = 2 * D_MODEL
LANES = 128
VMEM_LIMIT = 56 * 1024 * 1024

INT_MIN = -(2 ** 31)
NEG_BIG = -1e30


def _dot(a, b, precision=None):
    return jnp.dot(a, b, preferred_element_type=F32, precision=precision)


def _dot_nt(a, b, precision=None):
    return lax.dot_general(a, b, (((1,), (1,)), ((), ())), preferred_element_type=F32,
                           precision=precision)


def _dot_tn(a, b, precision=None):
    return lax.dot_general(a, b, (((0,), (0,)), ((), ())), preferred_element_type=F32,
                           precision=precision)


def _sigmoid(x):
    return 1.0 / (1.0 + jnp.exp(-x))


def _rope_group(xg, cos, sin_signed, lane_lo):
    partner = jnp.where(lane_lo, pltpu.roll(xg, LANES - HEAD_DIM // 2, 1),
                        pltpu.roll(xg, HEAD_DIM // 2, 1))
    return xg * cos + partner * sin_signed


def _proj_kernel(x_ref, g1_ref, wr_ref, wa_ref, wi_ref, wg_ref, mu_ref, cos_ref, sin_ref,
                 ps_ref, q_ref, k_ref, v_ref, qi_ref, ki_ref, kw_ref, gate_ref, carry_ref,
                 *, tiles_per_seq):
    i = pl.program_id(0)
    x = x_ref[...]
    tm = x.shape[0]
    u = x * lax.rsqrt(jnp.mean(x * x, axis=-1, keepdims=True) + NORM_EPS) * g1_ref[...]
    ub = u.astype(BF16)

    pr = _dot(ub, wr_ref[...])

    @pl.when(i % tiles_per_seq == 0)
    def _():
        carry_ref[...] = jnp.zeros_like(carry_ref)

    row = lax.broadcasted_iota(I32, (tm, 1), 0)
    prev = jnp.where(row == 0, carry_ref[7:8, :], pltpu.roll(pr, 1, 0))
    ps_ref[...] = pr + (prev - pr) * mu_ref[...]
    carry_ref[...] = pr[tm - 8:tm, :]

    cos = cos_ref[...]
    sin_signed = sin_ref[...]
    lane = lax.broadcasted_iota(I32, (1, LANES), 1)
    lane_lo = (lane % HEAD_DIM) < HEAD_DIM // 2

    pa = _dot(ub, wa_ref[...])
    for g in range(WIDTH // LANES):
        sl = slice(g * LANES, (g + 1) * LANES)
        qg = _rope_group(pa[:, sl], cos, sin_signed, lane_lo) * (HEAD_DIM ** -0.5)
        q_ref[:, sl] = qg.astype(BF16)
        kg = _rope_group(pa[:, WIDTH + g * LANES:WIDTH + (g + 1) * LANES], cos, sin_signed, lane_lo)
        k_ref[:, sl] = kg.astype(BF16)
    v_ref[...] = pa[:, 2 * WIDTH:].astype(BF16)

    pi = _dot(ub, wi_ref[...])
    for g in range(WIDTH // LANES):
        sl = slice(g * LANES, (g + 1) * LANES)
        qi_ref[:, sl] = _rope_group(pi[:, sl], cos, sin_signed, lane_lo).astype(BF16)
    tail = pi[:, WIDTH:WIDTH + LANES]
    kw_ref[...] = tail
    kr = _rope_group(tail, cos, sin_signed, lane_lo)
    ki_ref[...] = jnp.where(lane < HEAD_DIM, kr, pltpu.roll(kr, HEAD_DIM, 1)).astype(BF16)

    gate_ref[...] = _sigmoid(_dot(ub, wg_ref[...]))


def _rope_tables(seq):
    half = HEAD_DIM // 2
    inv = 1.0 / (ROPE_THETA ** (jnp.arange(half, dtype=F32) * 2.0 / HEAD_DIM))
    ang = jnp.arange(seq, dtype=F32)[:, None] * inv[None, :]
    cos, sin = jnp.cos(ang), jnp.sin(ang)
    cos_t = jnp.concatenate([cos, cos, cos, cos], axis=1)
    sin_t = jnp.concatenate([-sin, sin, -sin, sin], axis=1)
    return cos_t, sin_t


def _full(shape):
    return pl.BlockSpec(shape, lambda *_: (0,) * len(shape))


def _project(x2, norm1_g, w_in, tshift_mu, seq):
    tokens = x2.shape[0]
    tm = 256 if seq % 256 == 0 else seq
    c0, c1, c2 = RWKV_COLS, RWKV_COLS + ATT_COLS, RWKV_COLS + ATT_COLS + IDX_COLS
    wr = w_in[:, :c0].astype(BF16)
    wa = w_in[:, c0:c1].astype(BF16)
    wi = jnp.pad(w_in[:, c1:c2], ((0, 0), (0, IDX_PAD - IDX_COLS))).astype(BF16)
    wg = w_in[:, c2:].astype(BF16)
    cos_t, sin_t = _rope_tables(seq)
    tps = seq // tm
    row = lambda w: pl.BlockSpec((tm, w), lambda i: (i, 0))
    pos = pl.BlockSpec((tm, LANES), lambda i: (i % tps, 0))
    out_shape = (
        jax.ShapeDtypeStruct((tokens, RWKV_COLS), F32),
        jax.ShapeDtypeStruct((tokens, WIDTH), BF16),
        jax.ShapeDtypeStruct((tokens, WIDTH), BF16),
        jax.ShapeDtypeStruct((tokens, WIDTH), BF16),
        jax.ShapeDtypeStruct((tokens, WIDTH), BF16),
        jax.ShapeDtypeStruct((tokens, LANES), BF16),
        jax.ShapeDtypeStruct((tokens, LANES), F32),
        jax.ShapeDtypeStruct((tokens, GATE_COLS), F32),
    )
    return pl.pallas_call(
        functools.partial(_proj_kernel, tiles_per_seq=tps),
        grid=(tokens // tm,),
        in_specs=[row(D_MODEL), _full((1, D_MODEL)), _full((D_MODEL, RWKV_COLS)),
                  _full((D_MODEL, ATT_COLS)), _full((D_MODEL, IDX_PAD)), _full((D_MODEL, GATE_COLS)),
                  _full((1, RWKV_COLS)), pos, pos],
        out_specs=(row(RWKV_COLS), row(WIDTH), row(WIDTH), row(WIDTH), row(WIDTH),
                   row(LANES), row(LANES), row(GATE_COLS)),
        out_shape=out_shape,
        scratch_shapes=[pltpu.VMEM((8, RWKV_COLS), F32)],
        compiler_params=pltpu.CompilerParams(dimension_semantics=("arbitrary",),
                                             vmem_limit_bytes=VMEM_LIMIT),
        name="proj",
    )(x2, norm1_g.reshape(1, -1), wr, wa, wi, wg, tshift_mu.reshape(1, -1), cos_t, sin_t)


RWKV_CHUNK = 64
HI = lax.Precision.HIGHEST


def _pair_sum(x, lane_first):
    s0 = jnp.sum(jnp.where(lane_first, x, 0.0), axis=1, keepdims=True)
    s1 = jnp.sum(jnp.where(lane_first, 0.0, x), axis=1, keepdims=True)
    return jnp.where(lane_first, s0, s1)


def _rwkv_kernel(ps_ref, wdu_ref, w0_ref, aup_ref, a0_ref, gup_ref, kk_ref, ka_ref, rk_ref,
                 lng_ref, lnb_ref, y_ref, state_ref):
    c = pl.program_id(1)
    L = RWKV_CHUNK

    @pl.when(c == 0)
    def _():
        state_ref[...] = jnp.zeros_like(state_ref)

    ps = ps_ref[...]
    r_all = ps[:, :WIDTH]
    k_all = ps[:, WIDTH:2 * WIDTH]
    v_all = ps[:, 2 * WIDTH:3 * WIDTH]
    o = 3 * WIDTH
    wd = ps[:, o:o + DECAY_RANK]
    ad = ps[:, o + DECAY_RANK:o + DECAY_RANK + AAA_RANK]
    gd = ps[:, o + DECAY_RANK + AAA_RANK:]

    bf = lambda t: t.astype(BF16)
    z = -(w0_ref[...] + _dot(bf(jnp.tanh(wd)), wdu_ref[...]))
    softplus = jnp.maximum(z, 0.0) + jnp.log(1.0 + jnp.exp(-jnp.abs(z)))
    w_all = -softplus - 0.5
    ld_all = -jnp.exp(w_all)
    a_all = _sigmoid(a0_ref[...] + _dot(bf(ad), aup_ref[...]))
    g_all = _dot(bf(_sigmoid(gd)), gup_ref[...])

    lane = lax.broadcasted_iota(I32, (1, LANES), 1)
    lane_first = lane < HEAD_DIM
    ti = lax.broadcasted_iota(I32, (L, 2 * L), 0)
    tj = lax.broadcasted_iota(I32, (L, 2 * L), 1) % L
    strict = ti > tj
    incl = ti >= tj
    bi = lax.broadcasted_iota(I32, (LANES, 2 * LANES), 0)
    bj = lax.broadcasted_iota(I32, (LANES, 2 * LANES), 1) % LANES
    same_head = (bi < HEAD_DIM) == (bj < HEAD_DIM)
    eye = (bi == bj)[:, :LANES]
    tri_ones = jnp.where(incl[:, :L], 1.0, 0.0)
    zeros_l = jnp.zeros((L, LANES), BF16)

    n_pairs = WIDTH // LANES
    lane2_first = jnp.concatenate([lane_first, lane_first], axis=1)
    pr = []
    for g in range(n_pairs):
        sl = slice(g * LANES, (g + 1) * LANES)
        r, k, v = r_all[:, sl], k_all[:, sl], v_all[:, sl]
        a, ld = a_all[:, sl], ld_all[:, sl]
        kk = k * kk_ref[:, sl]
        nrm = jnp.sqrt(_pair_sum(kk * kk, lane_first))
        kk = kk / jnp.maximum(nrm, 1e-12)
        k = k * (1.0 + (a - 1.0) * ka_ref[:, sl])
        cum = _dot(tri_ones, ld, HI)
        p_inc = jnp.exp(cum)
        p_inv = jnp.exp(-cum)
        al = -kk * jnp.exp(cum - ld)
        be = kk * a * p_inv
        kt = k * p_inv
        rt = r * p_inc
        p_last = p_inc[L - 1:L, :]
        v_b = bf(v)
        pr.append(dict(
            r=r, k=k, v=v, al=al, rt=rt, p_last=p_last,
            bk=bf(jnp.concatenate([be, kt], axis=0)),
            bkp=bf(jnp.concatenate([be * p_last, kt * p_last], axis=0)),
            zv=jnp.concatenate([zeros_l, v_b], axis=0),
            zv2=jnp.concatenate([zeros_l, v_b], axis=1)))

    heads = [(g, hh) for g in range(n_pairs) for hh in range(2)]
    gram = []
    for g, hh in heads:
        mask = lane_first if hh == 0 else jnp.logical_not(lane_first)
        ar = bf(jnp.concatenate([jnp.where(mask, pr[g]["al"], 0.0),
                                 jnp.where(mask, pr[g]["rt"], 0.0)], axis=0))
        gram.append(_dot_nt(ar, pr[g]["bk"]))
    top = [jnp.where(strict, gm[:L], 0.0) for gm in gram]
    bot = [bf(jnp.where(incl, gm[L:], 0.0)) for gm in gram]
    rhs = [jnp.concatenate([pr[g]["al"], _dot(bf(top[i]), pr[g]["zv"])], axis=1)
           for i, (g, hh) in enumerate(heads)]
    npow = [bf(t[:, :L]) for t in top]
    u = [rhs[i] + _dot(npow[i], bf(rhs[i])) for i in range(len(heads))]
    for _ in range(int(math.log2(L)) - 1):
        npow = [bf(_dot(n, n)) for n in npow]
        u = [u[i] + _dot(npow[i], bf(u[i])) for i in range(len(heads))]
    u_b = [bf(x) for x in u]
    o_h = [_dot(bot[i], jnp.concatenate([u_b[i], pr[g]["zv2"]], axis=0))
           for i, (g, hh) in enumerate(heads)]

    for g in range(n_pairs):
        sl = slice(g * LANES, (g + 1) * LANES)
        p = pr[g]
        ub = jnp.where(lane2_first, u_b[2 * g], u_b[2 * g + 1])
        o2 = jnp.where(lane2_first, o_h[2 * g], o_h[2 * g + 1])
        ry = p["rt"] + o2[:, :LANES]
        yl = o2[:, LANES:]
        uz = jnp.concatenate([ub, p["zv2"]], axis=0)
        mg = jnp.where(same_head, _dot_tn(p["bkp"], uz), 0.0)
        m_mat = jnp.where(eye, p["p_last"], 0.0) + mg[:, :LANES]
        g_mat = mg[:, LANES:]

        st = state_ref[g]
        st_b = bf(st)
        y = _dot(bf(ry), st_b) + yl
        state_ref[g] = _dot(bf(m_mat), st_b) + g_mat

        mean = _pair_sum(y, lane_first) * (1.0 / HEAD_DIM)
        yc = y - mean
        var = _pair_sum(yc * yc, lane_first) * (1.0 / HEAD_DIM)
        yn = yc * lax.rsqrt(var + LNX_EPS) * lng_ref[:, sl] + lnb_ref[:, sl]
        bonus = _pair_sum(p["r"] * p["k"] * rk_ref[:, sl], lane_first) * p["v"]
        y_ref[:, sl] = (yn + bonus) * g_all[:, sl]


def _rwkv(ps, w_decay_up, w0, a_up, a0, g_up, k_k, k_a, r_k, lnx_g, lnx_b, batch, seq):
    L = RWKV_CHUNK
    n_chunks = seq // L
    vec = lambda t: t.reshape(1, WIDTH)
    return pl.pallas_call(
        _rwkv_kernel,
        grid=(batch, n_chunks),
        in_specs=[pl.BlockSpec((L, RWKV_COLS), lambda b, c: (b * n_chunks + c, 0)),
                  _full((DECAY_RANK, WIDTH)), _full((1, WIDTH)), _full((AAA_RANK, WIDTH)),
                  _full((1, WIDTH)), _full((GATE_RANK, WIDTH)), _full((1, WIDTH)),
                  _full((1, WIDTH)), _full((1, WIDTH)), _full((1, WIDTH)), _full((1, WIDTH))],
        out_specs=pl.BlockSpec((L, WIDTH), lambda b, c: (b * n_chunks + c, 0)),
        out_shape=jax.ShapeDtypeStruct((batch * seq, WIDTH), F32),
        scratch_shapes=[pltpu.VMEM((WIDTH // LANES, LANES, LANES), F32)],
        compiler_params=pltpu.CompilerParams(dimension_semantics=("arbitrary", "arbitrary"),
                                             vmem_limit_bytes=VMEM_LIMIT),
        name="rwkv",
    )(ps, w_decay_up.astype(BF16), vec(w0), a_up.astype(BF16), vec(a0), g_up.astype(BF16),
      vec(k_k), vec(k_a), vec(r_k), vec(lnx_g), vec(lnx_b))


DSA_TQ = LANES
DSA_KC = 512
DSA_RB = 512


def _dsa_kernel(qi_ref, kw_ref, q_ref, ki_ref, k_ref, v_ref, o_ref, keys_ref, acc_ref,
                s_ref, bias_ref, *, seq, topk, kc):
    tq = DSA_TQ
    qb = pl.program_id(1)
    n_kc = (qb * tq + tq + kc - 1) // kc
    t_col = qb * tq + lax.broadcasted_iota(I32, (1, tq), 1)
    row0 = lax.broadcasted_iota(I32, (kc, 1), 0)
    lane = lax.broadcasted_iota(I32, (1, LANES), 1)
    lane_first = lane < HEAD_DIM
    idx_scale = (HEAD_DIM ** -0.5) * (N_HEADS ** -0.5)
    zero = jnp.zeros((), BF16)

    def pair_stack(x):
        out = []
        for g in range(WIDTH // LANES):
            xg = x[:, g * LANES:(g + 1) * LANES]
            out.append(jnp.concatenate([jnp.where(lane_first, xg, zero),
                                        jnp.where(lane_first, zero, xg)], axis=0))
        return out

    qi2 = pair_stack(qi_ref[...])
    kw_t = kw_ref[...].T
    w_h = [kw_t[HEAD_DIM + h:HEAD_DIM + h + 1, :] * idx_scale for h in range(N_HEADS)]

    def score_chunk(c, carry):
        off = pl.multiple_of(c * kc, kc)
        kic = ki_ref[pl.ds(off, kc), :]
        acc = jnp.zeros((kc, tq), F32)
        for g in range(WIDTH // LANES):
            s2 = _dot_nt(kic, qi2[g])
            acc = acc + jnp.maximum(s2[:, :tq], 0.0) * w_h[2 * g]
            acc = acc + jnp.maximum(s2[:, tq:], 0.0) * w_h[2 * g + 1]
        bits = pltpu.bitcast(acc, I32)
        key = jnp.where(bits < 0, bits ^ jnp.int32(0x7FFFFFFF), bits)
        key = jnp.where(acc == 0.0, 0, key)
        key = jnp.where(off + row0 <= t_col, key, INT_MIN)
        keys_ref[pl.ds(off, kc), :] = key
        return carry

    lax.fori_loop(0, n_kc, score_chunk, 0)

    rb = DSA_RB
    n_blk = n_kc * (kc // rb)
    rrow = lax.broadcasted_iota(I32, (rb, 1), 0)

    def count_hits(hits):
        def body(j, cnt):
            blk = keys_ref[pl.ds(pl.multiple_of(j * rb, rb), rb), :]
            h = hits(blk, j)
            parts = [h[r * 8:(r + 1) * 8, :] for r in range(rb // 8)]
            while len(parts) > 1:
                parts = [parts[i] + parts[i + 1] for i in range(0, len(parts), 2)]
            return cnt + parts[0]
        cnt = lax.fori_loop(0, n_blk, body, jnp.zeros((8, tq), F32))
        return jnp.sum(cnt, axis=0, keepdims=True)

    def count(pred):
        return count_hits(lambda blk, j: jnp.where(pred(blk, j), 1.0, 0.0))

    kf = float(topk)
    n_pos = count(lambda blk, j: blk >= 0)
    thr0 = jnp.where(n_pos >= kf, 0, INT_MIN).astype(I32)
    done0 = jnp.where(n_pos == kf, 1.0, 0.0)

    def bisect_cond(st):
        i, _, done = st
        return jnp.logical_and(i < 31, jnp.min(done) < 0.5)

    def bisect_body(st):
        i, thr, done = st
        cand = thr | (jnp.int32(1) << (30 - i))
        n = count(lambda blk, j: blk >= cand)
        live = done < 0.5
        thr = jnp.where(jnp.logical_and(live, n >= kf), cand, thr)
        done = jnp.where(jnp.logical_and(live, n == kf), 1.0, done)
        return i + 1, thr, done

    _, thr, _ = lax.while_loop(bisect_cond, bisect_body, (jnp.int32(0), thr0, done0))

    n_gt = count(lambda blk, j: blk > thr)
    n_ge = count(lambda blk, j: blk >= thr)
    need = kf - n_gt
    idx_bits = max(1, int(math.ceil(math.log2(seq))))

    def tie_limit():
        def step(i, lim):
            cand = lim | (jnp.int32(1) << (idx_bits - 1 - i))
            below = count_hits(lambda blk, j: jnp.where(
                blk == thr, jnp.where(j * rb + rrow < cand, 1.0, 0.0), 0.0))
            return jnp.where(below < need, cand, lim)
        return lax.fori_loop(0, idx_bits, step, jnp.zeros((1, tq), I32))

    has_excess = jnp.max(n_ge - kf) > 0.0
    lim = lax.cond(has_excess, tie_limit, lambda: jnp.full((1, tq), seq, I32))

    q2 = pair_stack(q_ref[...])
    acc_ref[...] = jnp.zeros_like(acc_ref)
    n_pairs = WIDTH // LANES

    def make_bias(c):
        off = pl.multiple_of(c * kc, kc)
        key = keys_ref[pl.ds(off, kc), :]
        rowi = off + row0
        tie_ok = jnp.where(rowi <= lim, 0.0, NEG_BIG)
        bias = jnp.where(key > thr, 0.0, jnp.where(key == thr, tie_ok, NEG_BIG))
        bias_ref[...] = jnp.where(rowi <= t_col, bias, NEG_BIG)

    def logits(c, g):
        off = pl.multiple_of(c * kc, kc)
        kg = k_ref[pl.ds(off, kc), g * LANES:(g + 1) * LANES]
        bias = bias_ref[...]
        s2 = _dot_nt(kg, q2[g]) + jnp.concatenate([bias, bias], axis=1)
        s_ref[g] = s2
        return jnp.max(s2, axis=0, keepdims=True)

    def consume(c, g, m_old, l_old, m_chunk):
        off = pl.multiple_of(c * kc, kc)
        vg = v_ref[pl.ds(off, kc), g * LANES:(g + 1) * LANES]
        m_new = jnp.maximum(m_old, m_chunk)
        alpha = jnp.exp(m_old - m_new)
        p = jnp.exp(s_ref[g] - m_new)
        l_new = alpha * l_old + jnp.sum(p, axis=0, keepdims=True)
        acc_ref[g] = alpha * acc_ref[g] + _dot_tn(vg, p.astype(BF16))
        return m_new, l_new

    make_bias(0)
    mc0 = tuple(logits(0, g) for g in range(n_pairs))

    def attend(c, carry):
        ms, ls, mcs = carry
        make_bias(c)
        new_m, new_l, new_mc = [], [], []
        for g in range(n_pairs):
            m_new, l_new = consume(c - 1, g, ms[g], ls[g], mcs[g])
            new_mc.append(logits(c, g))
            new_m.append(m_new)
            new_l.append(l_new)
        return tuple(new_m), tuple(new_l), tuple(new_mc)

    init = (tuple(jnp.full((1, 2 * tq), NEG_BIG, F32) for _ in range(n_pairs)),
            tuple(jnp.zeros((1, 2 * tq), F32) for _ in range(n_pairs)), mc0)
    ms, ls, mcs = lax.fori_loop(1, n_kc, attend, init)
    ls = [consume(n_kc - 1, g, ms[g], ls[g], mcs[g])[1] for g in range(n_pairs)]

    for g in range(n_pairs):
        o2 = acc_ref[g] / ls[g]
        o_ref[:, g * LANES:(g + 1) * LANES] = jnp.where(lane_first, o2[:, :tq].T, o2[:, tq:].T)


def _dsa(qi, kw, q, ki, k, v, batch, seq):
    tq = DSA_TQ
    kc = DSA_KC if seq % DSA_KC == 0 else DSA_RB
    nqb = seq // tq
    topk = min(MAX_TOPK, seq // 4)
    qrow = lambda w: pl.BlockSpec((tq, w), lambda b, i: (b * nqb + i, 0))
    kvrow = lambda w: pl.BlockSpec((seq, w), lambda b, i: (b, 0))
    return pl.pallas_call(
        functools.partial(_dsa_kernel, seq=seq, topk=topk, kc=kc),
        grid=(batch, nqb),
        in_specs=[qrow(WIDTH), qrow(LANES), qrow(WIDTH), kvrow(LANES), kvrow(WIDTH), kvrow(WIDTH)],
        out_specs=qrow(WIDTH),
        out_shape=jax.ShapeDtypeStruct((batch * seq, WIDTH), F32),
        scratch_shapes=[pltpu.VMEM((seq, tq), I32),
                        pltpu.VMEM((WIDTH // LANES, LANES, 2 * tq), F32),
                        pltpu.VMEM((WIDTH // LANES, kc, 2 * tq), F32),
                        pltpu.VMEM((kc, tq), F32)],
        compiler_params=pltpu.CompilerParams(dimension_semantics=("arbitrary", "arbitrary"),
                                             vmem_limit_bytes=VMEM_LIMIT),
        name="dsa",
    )(qi, kw, q, ki, k, v)


def _tail_kernel(x_ref, ya_ref, yb_ref, gate_ref, wor_ref, woa_ref, wout_ref, g2_ref,
                 wfg_ref, wfu_ref, wfo_ref, gf_ref, o_ref):
    gate = gate_ref[...]
    ma = _dot(ya_ref[...].astype(BF16), wor_ref[...])
    mb = _dot(yb_ref[...].astype(BF16), woa_ref[...])
    merged = gate[:, :D_MODEL] * ma + gate[:, D_MODEL:] * mb
    h = x_ref[...] + _dot(merged.astype(BF16), wout_ref[...])
    hn = h * lax.rsqrt(jnp.mean(h * h, axis=-1, keepdims=True) + NORM_EPS) * g2_ref[...]
    hb = hn.astype(BF16)
    ffn = jnp.zeros_like(h)
    half = D_FF // 2
    for f in range(2):
        sl = slice(f * half, (f + 1) * half)
        zg = _dot(hb, wfg_ref[:, sl])
        zu = _dot(hb, wfu_ref[:, sl])
        act = zg * _sigmoid(zg) * zu
        ffn = ffn + _dot(act.astype(BF16), wfo_ref[sl, :])
    h = h + ffn
    o_ref[...] = h * lax.rsqrt(jnp.mean(h * h, axis=-1, keepdims=True) + NORM_EPS) * gf_ref[...]


def _tail(x2, ya, yb, gate, w_o_rwkv, w_o_att, w_out, norm2_g, w_ffn_in, w_ffn_out, normf_g):
    tokens = x2.shape[0]
    tm = 256 if tokens % 256 == 0 else tokens
    row = lambda w: pl.BlockSpec((tm, w), lambda i: (i, 0))
    const = lambda shape: pl.BlockSpec(shape, lambda i: (0, 0), pipeline_mode=pl.Buffered(1))
    return pl.pallas_call(
        _tail_kernel,
        grid=(tokens // tm,),
        in_specs=[row(D_MODEL), row(WIDTH), row(WIDTH), row(GATE_COLS),
                  const((WIDTH, D_MODEL)), const((WIDTH, D_MODEL)), const((D_MODEL, D_MODEL)),
                  const((1, D_MODEL)), const((D_MODEL, D_FF)), const((D_MODEL, D_FF)),
                  const((D_FF, D_MODEL)), const((1, D_MODEL))],
        out_specs=row(D_MODEL),
        out_shape=jax.ShapeDtypeStruct((tokens, D_MODEL), F32),
        compiler_params=pltpu.CompilerParams(dimension_semantics=("arbitrary",),
                                             vmem_limit_bytes=VMEM_LIMIT),
        name="tail",
    )(x2, ya, yb, gate, w_o_rwkv.astype(BF16), w_o_att.astype(BF16), w_out.astype(BF16),
      norm2_g.reshape(1, -1), w_ffn_in[:, :D_FF].astype(BF16), w_ffn_in[:, D_FF:].astype(BF16),
      w_ffn_out.astype(BF16), normf_g.reshape(1, -1))


def kernel(x, norm1_g, w_in, tshift_mu, w_decay_up, w0, a_up, a0, g_up, k_k, k_a, r_k, lnx_g, lnx_b, w_o_rwkv, w_o_att, w_out, norm2_g, w_ffn_in, w_ffn_out, normf_g):
    batch, seq, _ = x.shape
    x2 = x.reshape(batch * seq, D_MODEL)
    ps, q, k, v, qi, ki, kw, gate = _project(x2, norm1_g, w_in, tshift_mu, seq)
    ya = _rwkv(ps, w_decay_up, w0, a_up, a0, g_up, k_k, k_a, r_k.reshape(-1), lnx_g, lnx_b,
               batch, seq)
    yb = _dsa(qi, kw, q, ki, k, v, batch, seq)
    out = _tail(x2, ya, yb, gate, w_o_rwkv, w_o_att, w_out, norm2_g, w_ffn_in, w_ffn_out, normf_g)
    return out.reshape(batch, seq, D_MODEL)
```

```python
import functools
import math

import jax
import jax.numpy as jnp
import numpy as np
from jax import lax
from jax.experimental import pallas as pl
from jax.experimental.pallas import tpu as pltpu

F32 = jnp.float32
BF16 = jnp.bfloat16
I32 = jnp.int32

D_MODEL = 1024
HEAD_DIM = 64
N_HEADS = 8
WIDTH = N_HEADS * HEAD_DIM
DECAY_RANK = 64
AAA_RANK = 64
GATE_RANK = 128
MAX_TOPK = 256
ROPE_THETA = 10000.0
D_FF = 2816
NORM_EPS = 1e-6
LNX_EPS = 64e-5
RWKV_COLS = 3 * WIDTH + DECAY_RANK + AAA_RANK + GATE_RANK
ATT_COLS = 3 * WIDTH
IDX_COLS = N_HEADS * HEAD_DIM + HEAD_DIM + N_HEADS
IDX_PAD = 640
GATE_COLS = 2 * D_MODEL
LANES = 128
VMEM_LIMIT = 56 * 1024 * 1024

INT_MIN = -(2 ** 31)
LOG2_E = 1.4426950408889634
NEG_BIG = -1e30


def _dot(a, b, precision=None):
    return jnp.dot(a, b, preferred_element_type=F32, precision=precision)


def _dot_nt(a, b, precision=None):
    return lax.dot_general(a, b, (((1,), (1,)), ((), ())), preferred_element_type=F32,
                           precision=precision)


def _dot_tn(a, b, precision=None):
    return lax.dot_general(a, b, (((0,), (0,)), ((), ())), preferred_element_type=F32,
                           precision=precision)


def _sigmoid(x):
    return 1.0 / (1.0 + jnp.exp(-x))


def _rope_group(xg, cos, sin_signed, lane_lo):
    partner = jnp.where(lane_lo, pltpu.roll(xg, LANES - HEAD_DIM // 2, 1),
                        pltpu.roll(xg, HEAD_DIM // 2, 1))
    return xg * cos + partner * sin_signed


def _proj_kernel(x_ref, g1_ref, wr_ref, wa_ref, wi_ref, wg_ref, mu_ref, cos_ref, sin_ref,
                 ps_ref, q_ref, k_ref, v_ref, qi_ref, ki_ref, kw_ref, gate_ref, carry_ref,
                 *, tiles_per_seq):
    i = pl.program_id(0)
    x = x_ref[...]
    tm = x.shape[0]
    u = x * lax.rsqrt(jnp.mean(x * x, axis=-1, keepdims=True) + NORM_EPS) * g1_ref[...]
    ub = u.astype(BF16)

    pr = _dot(ub, wr_ref[...])

    @pl.when(i % tiles_per_seq == 0)
    def _():
        carry_ref[...] = jnp.zeros_like(carry_ref)

    row = lax.broadcasted_iota(I32, (tm, 1), 0)
    prev = jnp.where(row == 0, carry_ref[7:8, :], pltpu.roll(pr, 1, 0))
    ps_ref[...] = pr + (prev - pr) * mu_ref[...]
    carry_ref[...] = pr[tm - 8:tm, :]

    cos = cos_ref[...]
    sin_signed = sin_ref[...]
    lane = lax.broadcasted_iota(I32, (1, LANES), 1)
    lane_lo = (lane % HEAD_DIM) < HEAD_DIM // 2

    pa = _dot(ub, wa_ref[...])
    for g in range(WIDTH // LANES):
        sl = slice(g * LANES, (g + 1) * LANES)
        qg = _rope_group(pa[:, sl], cos, sin_signed, lane_lo) * (HEAD_DIM ** -0.5 * LOG2_E)
        q_ref[:, sl] = qg.astype(BF16)
        kg = _rope_group(pa[:, WIDTH + g * LANES:WIDTH + (g + 1) * LANES], cos, sin_signed, lane_lo)
        k_ref[:, sl] = kg.astype(BF16)
    v_ref[...] = pa[:, 2 * WIDTH:].astype(BF16)

    pi = _dot(ub, wi_ref[...])
    for g in range(WIDTH // LANES):
        sl = slice(g * LANES, (g + 1) * LANES)
        qi_ref[:, sl] = _rope_group(pi[:, sl], cos, sin_signed, lane_lo).astype(BF16)
    tail = pi[:, WIDTH:WIDTH + LANES]
    kw_ref[...] = tail
    kr = _rope_group(tail, cos, sin_signed, lane_lo)
    ki_ref[...] = jnp.where(lane < HEAD_DIM, kr, pltpu.roll(kr, HEAD_DIM, 1)).astype(BF16)

    gate_ref[...] = _sigmoid(_dot(ub, wg_ref[...]))


def _rope_tables(seq):
    half = HEAD_DIM // 2
    inv = 1.0 / (ROPE_THETA ** (jnp.arange(half, dtype=F32) * 2.0 / HEAD_DIM))
    ang = jnp.arange(seq, dtype=F32)[:, None] * inv[None, :]
    cos, sin = jnp.cos(ang), jnp.sin(ang)
    cos_t = jnp.concatenate([cos, cos, cos, cos], axis=1)
    sin_t = jnp.concatenate([-sin, sin, -sin, sin], axis=1)
    return cos_t, sin_t


def _full(shape):
    return pl.BlockSpec(shape, lambda *_: (0,) * len(shape))


def _project(x2, norm1_g, w_in, tshift_mu, seq):
    tokens = x2.shape[0]
    tm = 256 if seq % 256 == 0 else seq
    c0, c1, c2 = RWKV_COLS, RWKV_COLS + ATT_COLS, RWKV_COLS + ATT_COLS + IDX_COLS
    wr = w_in[:, :c0].astype(BF16)
    wa = w_in[:, c0:c1].astype(BF16)
    wi = jnp.pad(w_in[:, c1:c2], ((0, 0), (0, IDX_PAD - IDX_COLS))).astype(BF16)
    wg = w_in[:, c2:].astype(BF16)
    cos_t, sin_t = _rope_tables(seq)
    tps = seq // tm
    row = lambda w: pl.BlockSpec((tm, w), lambda i: (i, 0))
    pos = pl.BlockSpec((tm, LANES), lambda i: (i % tps, 0))
    out_shape = (
        jax.ShapeDtypeStruct((tokens, RWKV_COLS), F32),
        jax.ShapeDtypeStruct((tokens, WIDTH), BF16),
        jax.ShapeDtypeStruct((tokens, WIDTH), BF16),
        jax.ShapeDtypeStruct((tokens, WIDTH), BF16),
        jax.ShapeDtypeStruct((tokens, WIDTH), BF16),
        jax.ShapeDtypeStruct((tokens, LANES), BF16),
        jax.ShapeDtypeStruct((tokens, LANES), F32),
        jax.ShapeDtypeStruct((tokens, GATE_COLS), F32),
    )
    return pl.pallas_call(
        functools.partial(_proj_kernel, tiles_per_seq=tps),
        grid=(tokens // tm,),
        in_specs=[row(D_MODEL), _full((1, D_MODEL)), _full((D_MODEL, RWKV_COLS)),
                  _full((D_MODEL, ATT_COLS)), _full((D_MODEL, IDX_PAD)), _full((D_MODEL, GATE_COLS)),
                  _full((1, RWKV_COLS)), pos, pos],
        out_specs=(row(RWKV_COLS), row(WIDTH), row(WIDTH), row(WIDTH), row(WIDTH),
                   row(LANES), row(LANES), row(GATE_COLS)),
        out_shape=out_shape,
        scratch_shapes=[pltpu.VMEM((8, RWKV_COLS), F32)],
        compiler_params=pltpu.CompilerParams(dimension_semantics=("arbitrary",),
                                             vmem_limit_bytes=VMEM_LIMIT),
        name="proj",
    )(x2, norm1_g.reshape(1, -1), wr, wa, wi, wg, tshift_mu.reshape(1, -1), cos_t, sin_t)


RWKV_CHUNK = 64
HI = lax.Precision.HIGHEST


def _pair_sum(x, lane_first):
    s0 = jnp.sum(jnp.where(lane_first, x, 0.0), axis=1, keepdims=True)
    s1 = jnp.sum(jnp.where(lane_first, 0.0, x), axis=1, keepdims=True)
    return jnp.where(lane_first, s0, s1)


def _rwkv_kernel(ps_ref, wdu_ref, w0_ref, aup_ref, a0_ref, gup_ref, kk_ref, ka_ref, rk_ref,
                 lng_ref, lnb_ref, y_ref, state_ref):
    c = pl.program_id(1)
    L = RWKV_CHUNK

    @pl.when(c == 0)
    def _():
        state_ref[...] = jnp.zeros_like(state_ref)

    ps = ps_ref[...]
    r_all = ps[:, :WIDTH]
    k_all = ps[:, WIDTH:2 * WIDTH]
    v_all = ps[:, 2 * WIDTH:3 * WIDTH]
    o = 3 * WIDTH
    wd = ps[:, o:o + DECAY_RANK]
    ad = ps[:, o + DECAY_RANK:o + DECAY_RANK + AAA_RANK]
    gd = ps[:, o + DECAY_RANK + AAA_RANK:]

    bf = lambda t: t.astype(BF16)
    z = -(w0_ref[...] + _dot(bf(jnp.tanh(wd)), wdu_ref[...]))
    softplus = jnp.maximum(z, 0.0) + jnp.log(1.0 + jnp.exp(-jnp.abs(z)))
    w_all = -softplus - 0.5
    ld_all = -jnp.exp(w_all)
    a_all = _sigmoid(a0_ref[...] + _dot(bf(ad), aup_ref[...]))
    g_all = _dot(bf(_sigmoid(gd)), gup_ref[...])

    lane = lax.broadcasted_iota(I32, (1, LANES), 1)
    lane_first = lane < HEAD_DIM
    ti = lax.broadcasted_iota(I32, (L, 2 * L), 0)
    tj = lax.broadcasted_iota(I32, (L, 2 * L), 1) % L
    strict = ti > tj
    incl = ti >= tj
    bi = lax.broadcasted_iota(I32, (LANES, 2 * LANES), 0)
    bj = lax.broadcasted_iota(I32, (LANES, 2 * LANES), 1) % LANES
    same_head = (bi < HEAD_DIM) == (bj < HEAD_DIM)
    eye = (bi == bj)[:, :LANES]
    tri_ones = jnp.where(incl[:, :L], 1.0, 0.0)
    zeros_l = jnp.zeros((L, LANES), BF16)

    n_pairs = WIDTH // LANES
    lane2_first = jnp.concatenate([lane_first, lane_first], axis=1)
    pr = []
    for g in range(n_pairs):
        sl = slice(g * LANES, (g + 1) * LANES)
        r, k, v = r_all[:, sl], k_all[:, sl], v_all[:, sl]
        a, ld = a_all[:, sl], ld_all[:, sl]
        kk = k * kk_ref[:, sl]
        nrm = jnp.sqrt(_pair_sum(kk * kk, lane_first))
        kk = kk / jnp.maximum(nrm, 1e-12)
        k = k * (1.0 + (a - 1.0) * ka_ref[:, sl])
        cum = _dot(tri_ones, ld, HI)
        p_inc = jnp.exp(cum)
        p_inv = jnp.exp(-cum)
        al = -kk * jnp.exp(cum - ld)
        be = kk * a * p_inv
        kt = k * p_inv
        rt = r * p_inc
        p_last = p_inc[L - 1:L, :]
        v_b = bf(v)
        pr.append(dict(
            r=r, k=k, v=v, al=al, rt=rt, p_last=p_last,
            bk=bf(jnp.concatenate([be, kt], axis=0)),
            bkp=bf(jnp.concatenate([be * p_last, kt * p_last], axis=0)),
            zv=jnp.concatenate([zeros_l, v_b], axis=0),
            zv2=jnp.concatenate([zeros_l, v_b], axis=1)))

    heads = [(g, hh) for g in range(n_pairs) for hh in range(2)]
    gram = []
    for g, hh in heads:
        mask = lane_first if hh == 0 else jnp.logical_not(lane_first)
        ar = bf(jnp.concatenate([jnp.where(mask, pr[g]["al"], 0.0),
                                 jnp.where(mask, pr[g]["rt"], 0.0)], axis=0))
        gram.append(_dot_nt(ar, pr[g]["bk"]))
    top = [jnp.where(strict, gm[:L], 0.0) for gm in gram]
    bot = [bf(jnp.where(incl, gm[L:], 0.0)) for gm in gram]
    rhs = [jnp.concatenate([pr[g]["al"], _dot(bf(top[i]), pr[g]["zv"])], axis=1)
           for i, (g, hh) in enumerate(heads)]
    npow = [bf(t[:, :L]) for t in top]
    u = [rhs[i] + _dot(npow[i], bf(rhs[i])) for i in range(len(heads))]
    for _ in range(int(math.log2(L)) - 1):
        npow = [bf(_dot(n, n)) for n in npow]
        u = [u[i] + _dot(npow[i], bf(u[i])) for i in range(len(heads))]
    u_b = [bf(x) for x in u]
    o_h = [_dot(bot[i], jnp.concatenate([u_b[i], pr[g]["zv2"]], axis=0))
           for i, (g, hh) in enumerate(heads)]

    for g in range(n_pairs):
        sl = slice(g * LANES, (g + 1) * LANES)
        p = pr[g]
        ub = jnp.where(lane2_first, u_b[2 * g], u_b[2 * g + 1])
        o2 = jnp.where(lane2_first, o_h[2 * g], o_h[2 * g + 1])
        ry = p["rt"] + o2[:, :LANES]
        yl = o2[:, LANES:]
        uz = jnp.concatenate([ub, p["zv2"]], axis=0)
        mg = jnp.where(same_head, _dot_tn(p["bkp"], uz), 0.0)
        m_mat = jnp.where(eye, p["p_last"], 0.0) + mg[:, :LANES]
        g_mat = mg[:, LANES:]

        st = state_ref[g]
        st_b = bf(st)
        y = _dot(bf(ry), st_b) + yl
        state_ref[g] = _dot(bf(m_mat), st_b) + g_mat

        mean = _pair_sum(y, lane_first) * (1.0 / HEAD_DIM)
        yc = y - mean
        var = _pair_sum(yc * yc, lane_first) * (1.0 / HEAD_DIM)
        yn = yc * lax.rsqrt(var + LNX_EPS) * lng_ref[:, sl] + lnb_ref[:, sl]
        bonus = _pair_sum(p["r"] * p["k"] * rk_ref[:, sl], lane_first) * p["v"]
        y_ref[:, sl] = (yn + bonus) * g_all[:, sl]


def _rwkv(ps, w_decay_up, w0, a_up, a0, g_up, k_k, k_a, r_k, lnx_g, lnx_b, batch, seq):
    L = RWKV_CHUNK
    n_chunks = seq // L
    vec = lambda t: t.reshape(1, WIDTH)
    return pl.pallas_call(
        _rwkv_kernel,
        grid=(batch, n_chunks),
        in_specs=[pl.BlockSpec((L, RWKV_COLS), lambda b, c: (b * n_chunks + c, 0)),
                  _full((DECAY_RANK, WIDTH)), _full((1, WIDTH)), _full((AAA_RANK, WIDTH)),
                  _full((1, WIDTH)), _full((GATE_RANK, WIDTH)), _full((1, WIDTH)),
                  _full((1, WIDTH)), _full((1, WIDTH)), _full((1, WIDTH)), _full((1, WIDTH))],
        out_specs=pl.BlockSpec((L, WIDTH), lambda b, c: (b * n_chunks + c, 0)),
        out_shape=jax.ShapeDtypeStruct((batch * seq, WIDTH), F32),
        scratch_shapes=[pltpu.VMEM((WIDTH // LANES, LANES, LANES), F32)],
        compiler_params=pltpu.CompilerParams(dimension_semantics=("arbitrary", "arbitrary"),
                                             vmem_limit_bytes=VMEM_LIMIT),
        name="rwkv",
    )(ps, w_decay_up.astype(BF16), vec(w0), a_up.astype(BF16), vec(a0), g_up.astype(BF16),
      vec(k_k), vec(k_a), vec(r_k), vec(lnx_g), vec(lnx_b))


DSA_TQ = LANES
DSA_KC = 512
DSA_RB = 512
DSA_PASSES_PER_TEST = 4


def _dsa_kernel(qi_ref, kw_ref, q_ref, ki_ref, k_ref, v_ref, o_ref, keys_ref, acc_ref,
                s_ref, bias_ref, *, seq, topk, kc):
    tq = DSA_TQ
    qb = pl.program_id(1)
    n_kc = (qb * tq + tq + kc - 1) // kc
    t_col = qb * tq + lax.broadcasted_iota(I32, (1, tq), 1)
    row0 = lax.broadcasted_iota(I32, (kc, 1), 0)
    lane = lax.broadcasted_iota(I32, (1, LANES), 1)
    lane_first = lane < HEAD_DIM
    idx_scale = (HEAD_DIM ** -0.5) * (N_HEADS ** -0.5)
    zero = jnp.zeros((), BF16)
    n_pairs = WIDTH // LANES

    def pair_stack(x):
        out = []
        for g in range(WIDTH // LANES):
            xg = x[:, g * LANES:(g + 1) * LANES]
            out.append(jnp.concatenate([jnp.where(lane_first, xg, zero),
                                        jnp.where(lane_first, zero, xg)], axis=0))
        return out

    qi2 = pair_stack(qi_ref[...])
    kw_t = kw_ref[...].T
    w_h = [kw_t[HEAD_DIM + h:HEAD_DIM + h + 1, :] * idx_scale for h in range(N_HEADS)]

    def stage_scores(c, g):
        off = pl.multiple_of(c * kc, kc)
        s_ref[g] = _dot_nt(ki_ref[pl.ds(off, kc), :], qi2[g])

    def fold_scores(g, acc):
        s2 = s_ref[g]
        acc = acc + jnp.maximum(s2[:, :tq], 0.0) * w_h[2 * g]
        return acc + jnp.maximum(s2[:, tq:], 0.0) * w_h[2 * g + 1]

    def store_keys(c, acc):
        off = pl.multiple_of(c * kc, kc)
        bits = pltpu.bitcast(acc, I32)
        key = jnp.where(bits < 0, bits ^ jnp.int32(0x7FFFFFFF), bits)
        key = jnp.where(acc == 0.0, 0, key)
        key = jnp.where(off + row0 <= t_col, key, INT_MIN)
        keys_ref[pl.ds(off, kc), :] = key

    for g in range(n_pairs):
        stage_scores(0, g)

    def score_step(c, carry):
        acc = jnp.zeros((kc, tq), F32)
        for g in range(n_pairs):
            acc = fold_scores(g, acc)
            stage_scores(c, g)
        store_keys(c - 1, acc)
        return carry

    lax.fori_loop(1, n_kc, score_step, 0)
    acc = jnp.zeros((kc, tq), F32)
    for g in range(n_pairs):
        acc = fold_scores(g, acc)
    store_keys(n_kc - 1, acc)

    rb = DSA_RB
    n_blk = n_kc * (kc // rb)
    rrow = lax.broadcasted_iota(I32, (rb, 1), 0)

    def count_hits(hits):
        def body(j, cnt):
            blk = keys_ref[pl.ds(pl.multiple_of(j * rb, rb), rb), :]
            h = hits(blk, j)
            parts = [h[r * 8:(r + 1) * 8, :] for r in range(rb // 8)]
            while len(parts) > 1:
                parts = [parts[i] + parts[i + 1] for i in range(0, len(parts), 2)]
            return cnt + parts[0]
        cnt = lax.fori_loop(0, n_blk, body, jnp.zeros((8, tq), F32))
        return jnp.sum(cnt, axis=0, keepdims=True)

    def count(pred):
        return count_hits(lambda blk, j: jnp.where(pred(blk, j), 1.0, 0.0))

    kf = float(topk)
    n_pos = count(lambda blk, j: blk >= 0)
    thr0 = jnp.where(n_pos >= kf, 0, INT_MIN).astype(I32)
    done0 = jnp.where(n_pos == kf, 1.0, 0.0)

    def bisect_cond(st):
        i, _, done = st
        return jnp.logical_and(i < 31, jnp.min(done) < 0.5)

    def bisect_body(st):
        i, thr, done = st
        for _ in range(DSA_PASSES_PER_TEST):
            bit = jnp.where(i < 31, jnp.int32(1) << jnp.maximum(30 - i, 0), 0)
            cand = thr | bit
            n = count(lambda blk, j: blk >= cand)
            live = done < 0.5
            thr = jnp.where(jnp.logical_and(live, n >= kf), cand, thr)
            done = jnp.where(jnp.logical_and(live, n == kf), 1.0, done)
            i = i + 1
        return i, thr, done

    _, thr, _ = lax.while_loop(bisect_cond, bisect_body, (jnp.int32(0), thr0, done0))

    n_gt = count(lambda blk, j: blk > thr)
    n_ge = count(lambda blk, j: blk >= thr)
    need = kf - n_gt
    idx_bits = max(1, int(math.ceil(math.log2(seq))))

    def tie_limit():
        def step(i, lim):
            cand = lim | (jnp.int32(1) << (idx_bits - 1 - i))
            below = count_hits(lambda blk, j: jnp.where(
                blk == thr, jnp.where(j * rb + rrow < cand, 1.0, 0.0), 0.0))
            return jnp.where(below < need, cand, lim)
        return lax.fori_loop(0, idx_bits, step, jnp.zeros((1, tq), I32))

    has_excess = jnp.max(n_ge - kf) > 0.0
    lim = lax.cond(has_excess, tie_limit, lambda: jnp.full((1, tq), seq, I32))

    q2 = pair_stack(q_ref[...])

    def attention(plain):
        acc_ref[...] = jnp.zeros_like(acc_ref)

        def make_bias(c):
            off = pl.multiple_of(c * kc, kc)
            key = keys_ref[pl.ds(off, kc), :]
            if plain:
                bias_ref[...] = jnp.where(key >= thr, 0.0, NEG_BIG)
            else:
                rowi = off + row0
                tie_ok = jnp.where(rowi <= lim, 0.0, NEG_BIG)
                bias = jnp.where(key > thr, 0.0, jnp.where(key == thr, tie_ok, NEG_BIG))
                bias_ref[...] = jnp.where(rowi <= t_col, bias, NEG_BIG)

        def logits(c, g):
            off = pl.multiple_of(c * kc, kc)
            kg = k_ref[pl.ds(off, kc), g * LANES:(g + 1) * LANES]
            bias = bias_ref[...]
            s2 = _dot_nt(kg, q2[g]) + jnp.concatenate([bias, bias], axis=1)
            s_ref[g] = s2
            return jnp.max(s2, axis=0, keepdims=True)

        def consume(c, g, m_old, l_old, m_chunk):
            off = pl.multiple_of(c * kc, kc)
            vg = v_ref[pl.ds(off, kc), g * LANES:(g + 1) * LANES]
            m_new = jnp.maximum(m_old, m_chunk)
            alpha = jnp.exp2(m_old - m_new)
            p = jnp.exp2(s_ref[g] - m_new)
            l_new = alpha * l_old + jnp.sum(p, axis=0, keepdims=True)
            acc_ref[g] = alpha * acc_ref[g] + _dot_tn(vg, p.astype(BF16))
            return m_new, l_new

        make_bias(0)
        mc0 = tuple(logits(0, g) for g in range(n_pairs))

        def attend(c, carry):
            ms, ls, mcs = carry
            make_bias(c)
            new_m, new_l, new_mc = [], [], []
            for g in range(n_pairs):
                m_new, l_new = consume(c - 1, g, ms[g], ls[g], mcs[g])
                new_mc.append(logits(c, g))
                new_m.append(m_new)
                new_l.append(l_new)
            return tuple(new_m), tuple(new_l), tuple(new_mc)

        init = (tuple(jnp.full((1, 2 * tq), NEG_BIG, F32) for _ in range(n_pairs)),
                tuple(jnp.zeros((1, 2 * tq), F32) for _ in range(n_pairs)), mc0)
        ms, ls, mcs = lax.fori_loop(1, n_kc, attend, init)
        ls = [consume(n_kc - 1, g, ms[g], ls[g], mcs[g])[1] for g in range(n_pairs)]

        for g in range(n_pairs):
            o2 = acc_ref[g] / ls[g]
            o_ref[:, g * LANES:(g + 1) * LANES] = jnp.where(lane_first, o2[:, :tq].T,
                                                            o2[:, tq:].T)

    plain = jnp.logical_and(jnp.logical_not(has_excess), qb * tq + 1 >= topk)
    pl.when(plain)(lambda: attention(True))
    pl.when(jnp.logical_not(plain))(lambda: attention(False))


def _dsa(qi, kw, q, ki, k, v, batch, seq):
    tq = DSA_TQ
    kc = DSA_KC if seq % DSA_KC == 0 else DSA_RB
    nqb = seq // tq
    topk = min(MAX_TOPK, seq // 4)
    qrow = lambda w: pl.BlockSpec((tq, w), lambda b, i: (b * nqb + i, 0))
    kvrow = lambda w: pl.BlockSpec((seq, w), lambda b, i: (b, 0))
    return pl.pallas_call(
        functools.partial(_dsa_kernel, seq=seq, topk=topk, kc=kc),
        grid=(batch, nqb),
        in_specs=[qrow(WIDTH), qrow(LANES), qrow(WIDTH), kvrow(LANES), kvrow(WIDTH), kvrow(WIDTH)],
        out_specs=qrow(WIDTH),
        out_shape=jax.ShapeDtypeStruct((batch * seq, WIDTH), F32),
        scratch_shapes=[pltpu.VMEM((seq, tq), I32),
                        pltpu.VMEM((WIDTH // LANES, LANES, 2 * tq), F32),
                        pltpu.VMEM((WIDTH // LANES, kc, 2 * tq), F32),
                        pltpu.VMEM((kc, tq), F32)],
        compiler_params=pltpu.CompilerParams(dimension_semantics=("arbitrary", "arbitrary"),
                                             vmem_limit_bytes=VMEM_LIMIT),
        name="dsa",
    )(qi, kw, q, ki, k, v)


def _tail_kernel(x_ref, ya_ref, yb_ref, gate_ref, wor_ref, woa_ref, wout_ref, g2_ref,
                 wfg_ref, wfu_ref, wfo_ref, gf_ref, o_ref):
    gate = gate_ref[...]
    ma = _dot(ya_ref[...].astype(BF16), wor_ref[...])
    mb = _dot(yb_ref[...].astype(BF16), woa_ref[...])
    merged = gate[:, :D_MODEL] * ma + gate[:, D_MODEL:] * mb
    h = x_ref[...] + _dot(merged.astype(BF16), wout_ref[...])
    hn = h * lax.rsqrt(jnp.mean(h * h, axis=-1, keepdims=True) + NORM_EPS) * g2_ref[...]
    hb = hn.astype(BF16)
    ffn = jnp.zeros_like(h)
    half = D_FF // 2
    for f in range(2):
        sl = slice(f * half, (f + 1) * half)
        zg = _dot(hb, wfg_ref[:, sl])
        zu = _dot(hb, wfu_ref[:, sl])
        act = zg * _sigmoid(zg) * zu
        ffn = ffn + _dot(act.astype(BF16), wfo_ref[sl, :])
    h = h + ffn
    o_ref[...] = h * lax.rsqrt(jnp.mean(h * h, axis=-1, keepdims=True) + NORM_EPS) * gf_ref[...]


def _tail(x2, ya, yb, gate, w_o_rwkv, w_o_att, w_out, norm2_g, w_ffn_in, w_ffn_out, normf_g):
    tokens = x2.shape[0]
    tm = 256 if tokens % 256 == 0 else tokens
    row = lambda w: pl.BlockSpec((tm, w), lambda i: (i, 0))
    const = lambda shape: pl.BlockSpec(shape, lambda i: (0, 0), pipeline_mode=pl.Buffered(1))
    return pl.pallas_call(
        _tail_kernel,
        grid=(tokens // tm,),
        in_specs=[row(D_MODEL), row(WIDTH), row(WIDTH), row(GATE_COLS),
                  const((WIDTH, D_MODEL)), const((WIDTH, D_MODEL)), const((D_MODEL, D_MODEL)),
                  const((1, D_MODEL)), const((D_MODEL, D_FF)), const((D_MODEL, D_FF)),
                  const((D_FF, D_MODEL)), const((1, D_MODEL))],
        out_specs=row(D_MODEL),
        out_shape=jax.ShapeDtypeStruct((tokens, D_MODEL), F32),
        compiler_params=pltpu.CompilerParams(dimension_semantics=("arbitrary",),
                                             vmem_limit_bytes=VMEM_LIMIT),
        name="tail",
    )(x2, ya, yb, gate, w_o_rwkv.astype(BF16), w_o_att.astype(BF16), w_out.astype(BF16),
      norm2_g.reshape(1, -1), w_ffn_in[:, :D_FF].astype(BF16), w_ffn_in[:, D_FF:].astype(BF16),
      w_ffn_out.astype(BF16), normf_g.reshape(1, -1))


def kernel(x, norm1_g, w_in, tshift_mu, w_decay_up, w0, a_up, a0, g_up, k_k, k_a, r_k, lnx_g, lnx_b, w_o_rwkv, w_o_att, w_out, norm2_g, w_ffn_in, w_ffn_out, normf_g):
    batch, seq, _ = x.shape
    x2 = x.reshape(batch * seq, D_MODEL)
    ps, q, k, v, qi, ki, kw, gate = _project(x2, norm1_g, w_in, tshift_mu, seq)
    ya = _rwkv(ps, w_decay_up, w0, a_up, a0, g_up, k_k, k_a, r_k.reshape(-1), lnx_g, lnx_b,
               batch, seq)
    yb = _dsa(qi, kw, q, ki, k, v, batch, seq)
    out = _tail(x2, ya, yb, gate, w_o_rwkv, w_o_att, w_out, norm2_g, w_ffn_in, w_ffn_out, normf_g)
    return out.reshape(batch, seq, D_MODEL)
```

```python
import functools
import math

import jax
import jax.numpy as jnp
import numpy as np
from jax import lax
from jax.experimental import pallas as pl
from jax.experimental.pallas import tpu as pltpu

F32 = jnp.float32
BF16 = jnp.bfloat16
I32 = jnp.int32

D_MODEL = 1024
HEAD_DIM = 64
N_HEADS = 8
WIDTH = N_HEADS * HEAD_DIM
DECAY_RANK = 64
AAA_RANK = 64
GATE_RANK = 128
MAX_TOPK = 256
ROPE_THETA = 10000.0
D_FF = 2816
NORM_EPS = 1e-6
LNX_EPS = 64e-5
RWKV_COLS = 3 * WIDTH + DECAY_RANK + AAA_RANK + GATE_RANK
ATT_COLS = 3 * WIDTH
IDX_COLS = N_HEADS * HEAD_DIM + HEAD_DIM + N_HEADS
IDX_PAD = 640
GATE_COLS = 2 * D_MODEL
LANES = 128
VMEM_LIMIT = 56 * 1024 * 1024

INT_MIN = -(2 ** 31)
LOG2_E = 1.4426950408889634
NEG_BIG = -1e30


def _dot(a, b, precision=None):
    return jnp.dot(a, b, preferred_element_type=F32, precision=precision)


def _dot_nt(a, b, precision=None):
    return lax.dot_general(a, b, (((1,), (1,)), ((), ())), preferred_element_type=F32,
                           precision=precision)


def _dot_tn(a, b, precision=None):
    return lax.dot_general(a, b, (((0,), (0,)), ((), ())), preferred_element_type=F32,
                           precision=precision)


def _sigmoid(x):
    return 1.0 / (1.0 + jnp.exp(-x))


def _rope_group(xg, cos, sin_signed, lane_lo):
    partner = jnp.where(lane_lo, pltpu.roll(xg, LANES - HEAD_DIM // 2, 1),
                        pltpu.roll(xg, HEAD_DIM // 2, 1))
    return xg * cos + partner * sin_signed


def _proj_kernel(x_ref, g1_ref, wr_ref, wa_ref, wi_ref, wg_ref, mu_ref, cos_ref, sin_ref,
                 ps_ref, q_ref, k_ref, v_ref, qi_ref, ki_ref, kw_ref, gate_ref, carry_ref,
                 *, tiles_per_seq):
    i = pl.program_id(0)
    x = x_ref[...]
    tm = x.shape[0]
    u = x * lax.rsqrt(jnp.mean(x * x, axis=-1, keepdims=True) + NORM_EPS) * g1_ref[...]
    ub = u.astype(BF16)

    pr = _dot(ub, wr_ref[...])

    @pl.when(i % tiles_per_seq == 0)
    def _():
        carry_ref[...] = jnp.zeros_like(carry_ref)

    row = lax.broadcasted_iota(I32, (tm, 1), 0)
    prev = jnp.where(row == 0, carry_ref[7:8, :], pltpu.roll(pr, 1, 0))
    ps_ref[...] = pr + (prev - pr) * mu_ref[...]
    carry_ref[...] = pr[tm - 8:tm, :]

    cos = cos_ref[...]
    sin_signed = sin_ref[...]
    lane = lax.broadcasted_iota(I32, (1, LANES), 1)
    lane_lo = (lane % HEAD_DIM) < HEAD_DIM // 2

    pa = _dot(ub, wa_ref[...])
    for g in range(WIDTH // LANES):
        sl = slice(g * LANES, (g + 1) * LANES)
        qg = _rope_group(pa[:, sl], cos, sin_signed, lane_lo) * (HEAD_DIM ** -0.5 * LOG2_E)
        q_ref[:, sl] = qg.astype(BF16)
        kg = _rope_group(pa[:, WIDTH + g * LANES:WIDTH + (g + 1) * LANES], cos, sin_signed, lane_lo)
        k_ref[:, sl] = kg.astype(BF16)
    v_ref[...] = pa[:, 2 * WIDTH:].astype(BF16)

    pi = _dot(ub, wi_ref[...])
    for g in range(WIDTH // LANES):
        sl = slice(g * LANES, (g + 1) * LANES)
        qi_ref[:, sl] = _rope_group(pi[:, sl], cos, sin_signed, lane_lo).astype(BF16)
    tail = pi[:, WIDTH:WIDTH + LANES]
    kw_ref[...] = tail
    kr = _rope_group(tail, cos, sin_signed, lane_lo)
    ki_ref[...] = jnp.where(lane < HEAD_DIM, kr, pltpu.roll(kr, HEAD_DIM, 1)).astype(BF16)

    gate_ref[...] = _sigmoid(_dot(ub, wg_ref[...]))


def _rope_tables(seq):
    half = HEAD_DIM // 2
    inv = 1.0 / (ROPE_THETA ** (jnp.arange(half, dtype=F32) * 2.0 / HEAD_DIM))
    ang = jnp.arange(seq, dtype=F32)[:, None] * inv[None, :]
    cos, sin = jnp.cos(ang), jnp.sin(ang)
    cos_t = jnp.concatenate([cos, cos, cos, cos], axis=1)
    sin_t = jnp.concatenate([-sin, sin, -sin, sin], axis=1)
    return cos_t, sin_t


def _full(shape):
    return pl.BlockSpec(shape, lambda *_: (0,) * len(shape))


def _project(x2, norm1_g, w_in, tshift_mu, seq):
    tokens = x2.shape[0]
    tm = 512 if seq % 512 == 0 else seq
    c0, c1, c2 = RWKV_COLS, RWKV_COLS + ATT_COLS, RWKV_COLS + ATT_COLS + IDX_COLS
    wr = w_in[:, :c0].astype(BF16)
    wa = w_in[:, c0:c1].astype(BF16)
    wi = jnp.pad(w_in[:, c1:c2], ((0, 0), (0, IDX_PAD - IDX_COLS))).astype(BF16)
    wg = w_in[:, c2:].astype(BF16)
    cos_t, sin_t = _rope_tables(seq)
    tps = seq // tm
    row = lambda w: pl.BlockSpec((tm, w), lambda i: (i, 0))
    pos = pl.BlockSpec((tm, LANES), lambda i: (i % tps, 0))
    out_shape = (
        jax.ShapeDtypeStruct((tokens, RWKV_COLS), F32),
        jax.ShapeDtypeStruct((tokens, WIDTH), BF16),
        jax.ShapeDtypeStruct((tokens, WIDTH), BF16),
        jax.ShapeDtypeStruct((tokens, WIDTH), BF16),
        jax.ShapeDtypeStruct((tokens, WIDTH), BF16),
        jax.ShapeDtypeStruct((tokens, LANES), BF16),
        jax.ShapeDtypeStruct((tokens, LANES), F32),
        jax.ShapeDtypeStruct((tokens, GATE_COLS), F32),
    )
    return pl.pallas_call(
        functools.partial(_proj_kernel, tiles_per_seq=tps),
        grid=(tokens // tm,),
        in_specs=[row(D_MODEL), _full((1, D_MODEL)), _full((D_MODEL, RWKV_COLS)),
                  _full((D_MODEL, ATT_COLS)), _full((D_MODEL, IDX_PAD)), _full((D_MODEL, GATE_COLS)),
                  _full((1, RWKV_COLS)), pos, pos],
        out_specs=(row(RWKV_COLS), row(WIDTH), row(WIDTH), row(WIDTH), row(WIDTH),
                   row(LANES), row(LANES), row(GATE_COLS)),
        out_shape=out_shape,
        scratch_shapes=[pltpu.VMEM((8, RWKV_COLS), F32)],
        compiler_params=pltpu.CompilerParams(dimension_semantics=("arbitrary",),
                                             vmem_limit_bytes=VMEM_LIMIT),
        name="proj",
    )(x2, norm1_g.reshape(1, -1), wr, wa, wi, wg, tshift_mu.reshape(1, -1), cos_t, sin_t)


RWKV_CHUNK = 64
RWKV_CHUNKS_PER_STEP = 2
HI = lax.Precision.HIGHEST


def _pair_sum(x, lane_first):
    s0 = jnp.sum(jnp.where(lane_first, x, 0.0), axis=1, keepdims=True)
    s1 = jnp.sum(jnp.where(lane_first, 0.0, x), axis=1, keepdims=True)
    return jnp.where(lane_first, s0, s1)


def _rwkv_kernel(ps_ref, wdu_ref, w0_ref, aup_ref, a0_ref, gup_ref, kk_ref, ka_ref, rk_ref,
                 lng_ref, lnb_ref, y_ref, state_ref):
    c = pl.program_id(1)
    L = RWKV_CHUNK

    @pl.when(c == 0)
    def _():
        state_ref[...] = jnp.zeros_like(state_ref)

    ps = ps_ref[...]
    r_all = ps[:, :WIDTH]
    k_all = ps[:, WIDTH:2 * WIDTH]
    v_all = ps[:, 2 * WIDTH:3 * WIDTH]
    o = 3 * WIDTH
    wd = ps[:, o:o + DECAY_RANK]
    ad = ps[:, o + DECAY_RANK:o + DECAY_RANK + AAA_RANK]
    gd = ps[:, o + DECAY_RANK + AAA_RANK:]

    bf = lambda t: t.astype(BF16)
    z = -(w0_ref[...] + _dot(bf(jnp.tanh(wd)), wdu_ref[...]))
    softplus = jnp.maximum(z, 0.0) + jnp.log(1.0 + jnp.exp(-jnp.abs(z)))
    w_all = -softplus - 0.5
    ld_all = -jnp.exp(w_all)
    a_all = _sigmoid(a0_ref[...] + _dot(bf(ad), aup_ref[...]))
    g_all = _dot(bf(_sigmoid(gd)), gup_ref[...])

    lane = lax.broadcasted_iota(I32, (1, LANES), 1)
    lane_first = lane < HEAD_DIM
    ti = lax.broadcasted_iota(I32, (L, 2 * L), 0)
    tj = lax.broadcasted_iota(I32, (L, 2 * L), 1) % L
    strict = ti > tj
    incl = ti >= tj
    bi = lax.broadcasted_iota(I32, (LANES, 2 * LANES), 0)
    bj = lax.broadcasted_iota(I32, (LANES, 2 * LANES), 1) % LANES
    same_head = (bi < HEAD_DIM) == (bj < HEAD_DIM)
    eye = (bi == bj)[:, :LANES]
    tri_ones = jnp.where(incl[:, :L], 1.0, 0.0)
    zeros_l = jnp.zeros((L, LANES), BF16)

    n_pairs = WIDTH // LANES
    lane2_first = jnp.concatenate([lane_first, lane_first], axis=1)
    pr = {}
    for ci in range(RWKV_CHUNKS_PER_STEP):
        rows = slice(ci * L, (ci + 1) * L)
        for g in range(n_pairs):
            sl = slice(g * LANES, (g + 1) * LANES)
            r, k, v = r_all[rows, sl], k_all[rows, sl], v_all[rows, sl]
            a, ld = a_all[rows, sl], ld_all[rows, sl]
            kk = k * kk_ref[:, sl]
            nrm = jnp.sqrt(_pair_sum(kk * kk, lane_first))
            kk = kk / jnp.maximum(nrm, 1e-12)
            k = k * (1.0 + (a - 1.0) * ka_ref[:, sl])
            cum = _dot(tri_ones, ld, HI)
            p_inc = jnp.exp(cum)
            p_inv = jnp.exp(-cum)
            al = -kk * jnp.exp(cum - ld)
            be = kk * a * p_inv
            kt = k * p_inv
            rt = r * p_inc
            p_last = p_inc[L - 1:L, :]
            v_b = bf(v)
            pr[ci, g] = dict(
                r=r, k=k, v=v, al=al, rt=rt, p_last=p_last,
                bk=bf(jnp.concatenate([be, kt], axis=0)),
                bkp=bf(jnp.concatenate([be * p_last, kt * p_last], axis=0)),
                zv=jnp.concatenate([zeros_l, v_b], axis=0),
                zv2=jnp.concatenate([zeros_l, v_b], axis=1))

    heads = [(ci, g, hh) for ci in range(RWKV_CHUNKS_PER_STEP) for g in range(n_pairs)
             for hh in range(2)]
    gram = []
    for ci, g, hh in heads:
        mask = lane_first if hh == 0 else jnp.logical_not(lane_first)
        ar = bf(jnp.concatenate([jnp.where(mask, pr[ci, g]["al"], 0.0),
                                 jnp.where(mask, pr[ci, g]["rt"], 0.0)], axis=0))
        gram.append(_dot_nt(ar, pr[ci, g]["bk"]))
    top = [jnp.where(strict, gm[:L], 0.0) for gm in gram]
    bot = [bf(jnp.where(incl, gm[L:], 0.0)) for gm in gram]
    rhs = [jnp.concatenate([pr[ci, g]["al"], _dot(bf(top[i]), pr[ci, g]["zv"])], axis=1)
           for i, (ci, g, hh) in enumerate(heads)]
    npow = [bf(t[:, :L]) for t in top]
    u = [rhs[i] + _dot(npow[i], bf(rhs[i])) for i in range(len(heads))]
    for _ in range(int(math.log2(L)) - 1):
        npow = [bf(_dot(n, n)) for n in npow]
        u = [u[i] + _dot(npow[i], bf(u[i])) for i in range(len(heads))]
    u_b = [bf(x) for x in u]
    o_h = [_dot(bot[i], jnp.concatenate([u_b[i], pr[ci, g]["zv2"]], axis=0))
           for i, (ci, g, hh) in enumerate(heads)]

    for g in range(n_pairs):
        sl = slice(g * LANES, (g + 1) * LANES)
        st = state_ref[g]
        for ci in range(RWKV_CHUNKS_PER_STEP):
            rows = slice(ci * L, (ci + 1) * L)
            p = pr[ci, g]
            i0 = (ci * n_pairs + g) * 2
            ub = jnp.where(lane2_first, u_b[i0], u_b[i0 + 1])
            o2 = jnp.where(lane2_first, o_h[i0], o_h[i0 + 1])
            ry = p["rt"] + o2[:, :LANES]
            yl = o2[:, LANES:]
            uz = jnp.concatenate([ub, p["zv2"]], axis=0)
            mg = jnp.where(same_head, _dot_tn(p["bkp"], uz), 0.0)
            m_mat = jnp.where(eye, p["p_last"], 0.0) + mg[:, :LANES]
            g_mat = mg[:, LANES:]

            st_b = bf(st)
            y = _dot(bf(ry), st_b) + yl
            st = _dot(bf(m_mat), st_b) + g_mat

            mean = _pair_sum(y, lane_first) * (1.0 / HEAD_DIM)
            yc = y - mean
            var = _pair_sum(yc * yc, lane_first) * (1.0 / HEAD_DIM)
            yn = yc * lax.rsqrt(var + LNX_EPS) * lng_ref[:, sl] + lnb_ref[:, sl]
            bonus = _pair_sum(p["r"] * p["k"] * rk_ref[:, sl], lane_first) * p["v"]
            y_ref[rows, sl] = (yn + bonus) * g_all[rows, sl]
        state_ref[g] = st


def _rwkv(ps, w_decay_up, w0, a_up, a0, g_up, k_k, k_a, r_k, lnx_g, lnx_b, batch, seq):
    L = RWKV_CHUNK * RWKV_CHUNKS_PER_STEP
    n_chunks = seq // L
    vec = lambda t: t.reshape(1, WIDTH)
    return pl.pallas_call(
        _rwkv_kernel,
        grid=(batch, n_chunks),
        in_specs=[pl.BlockSpec((L, RWKV_COLS), lambda b, c: (b * n_chunks + c, 0)),
                  _full((DECAY_RANK, WIDTH)), _full((1, WIDTH)), _full((AAA_RANK, WIDTH)),
                  _full((1, WIDTH)), _full((GATE_RANK, WIDTH)), _full((1, WIDTH)),
                  _full((1, WIDTH)), _full((1, WIDTH)), _full((1, WIDTH)), _full((1, WIDTH))],
        out_specs=pl.BlockSpec((L, WIDTH), lambda b, c: (b * n_chunks + c, 0)),
        out_shape=jax.ShapeDtypeStruct((batch * seq, WIDTH), F32),
        scratch_shapes=[pltpu.VMEM((WIDTH // LANES, LANES, LANES), F32)],
        compiler_params=pltpu.CompilerParams(dimension_semantics=("arbitrary", "arbitrary"),
                                             vmem_limit_bytes=VMEM_LIMIT),
        name="rwkv",
    )(ps, w_decay_up.astype(BF16), vec(w0), a_up.astype(BF16), vec(a0), g_up.astype(BF16),
      vec(k_k), vec(k_a), vec(r_k), vec(lnx_g), vec(lnx_b))


DSA_TQ = LANES
DSA_KC = 512
DSA_RB = 512
DSA_PASSES_PER_TEST = 4


def _dsa_kernel(qi_ref, kw_ref, q_ref, ki_ref, k_ref, v_ref, o_ref, keys_ref, acc_ref,
                s_ref, bias_ref, coarse_ref, *, seq, topk, kc):
    tq = DSA_TQ
    qb = pl.program_id(1)
    n_kc = (qb * tq + tq + kc - 1) // kc
    t_col = qb * tq + lax.broadcasted_iota(I32, (1, tq), 1)
    row0 = lax.broadcasted_iota(I32, (kc, 1), 0)
    lane = lax.broadcasted_iota(I32, (1, LANES), 1)
    lane_first = lane < HEAD_DIM
    idx_scale = (HEAD_DIM ** -0.5) * (N_HEADS ** -0.5)
    zero = jnp.zeros((), BF16)
    n_pairs = WIDTH // LANES

    def pair_stack(x):
        out = []
        for g in range(WIDTH // LANES):
            xg = x[:, g * LANES:(g + 1) * LANES]
            out.append(jnp.concatenate([jnp.where(lane_first, xg, zero),
                                        jnp.where(lane_first, zero, xg)], axis=0))
        return out

    qi2 = pair_stack(qi_ref[...])
    kw_t = kw_ref[...].T
    w_h = [kw_t[HEAD_DIM + h:HEAD_DIM + h + 1, :] * idx_scale for h in range(N_HEADS)]

    def stage_scores(c, g):
        off = pl.multiple_of(c * kc, kc)
        s_ref[g] = _dot_nt(ki_ref[pl.ds(off, kc), :], qi2[g])

    def fold_scores(g, acc):
        s2 = s_ref[g]
        acc = acc + jnp.maximum(s2[:, :tq], 0.0) * w_h[2 * g]
        return acc + jnp.maximum(s2[:, tq:], 0.0) * w_h[2 * g + 1]

    def store_keys(c, acc):
        off = pl.multiple_of(c * kc, kc)
        bits = pltpu.bitcast(acc, I32)
        key = jnp.where(bits < 0, bits ^ jnp.int32(0x7FFFFFFF), bits)
        key = jnp.where(acc == 0.0, 0, key)
        causal = off + row0 <= t_col
        keys_ref[pl.ds(off, kc), :] = jnp.where(causal, key, INT_MIN)
        bits = jnp.where(acc == 0.0, 0, bits) & jnp.int32(-65536)
        coarse = jnp.where(causal, pltpu.bitcast(bits, F32), -jnp.inf)
        coarse_ref[pl.ds(off, kc), :] = coarse.astype(BF16)

    for g in range(n_pairs):
        stage_scores(0, g)

    def score_step(c, carry):
        acc = jnp.zeros((kc, tq), F32)
        for g in range(n_pairs):
            acc = fold_scores(g, acc)
            stage_scores(c, g)
        store_keys(c - 1, acc)
        return carry

    lax.fori_loop(1, n_kc, score_step, 0)
    acc = jnp.zeros((kc, tq), F32)
    for g in range(n_pairs):
        acc = fold_scores(g, acc)
    store_keys(n_kc - 1, acc)

    rb = DSA_RB
    n_blk = n_kc * (kc // rb)
    rrow = lax.broadcasted_iota(I32, (rb, 1), 0)

    def count_hits(hits):
        def body(j, cnt):
            blk = keys_ref[pl.ds(pl.multiple_of(j * rb, rb), rb), :]
            h = hits(blk, j)
            parts = [h[r * 8:(r + 1) * 8, :] for r in range(rb // 8)]
            while len(parts) > 1:
                parts = [parts[i] + parts[i + 1] for i in range(0, len(parts), 2)]
            return cnt + parts[0]
        cnt = lax.fori_loop(0, n_blk, body, jnp.zeros((8, tq), F32))
        return jnp.sum(cnt, axis=0, keepdims=True)

    def count(pred):
        return count_hits(lambda blk, j: jnp.where(pred(blk, j), 1.0, 0.0))

    def count_coarse(cand):
        fbits = jnp.where(cand < 0, cand ^ jnp.int32(0x7FFFFFFF), cand) & jnp.int32(-65536)
        min_normal = jnp.int32(0x00800000)
        fbits = jnp.where(jnp.logical_and(fbits > 0, fbits < min_normal), min_normal, fbits)
        cf = pltpu.bitcast(fbits, F32).astype(BF16)
        one, nil = jnp.ones((), BF16), jnp.zeros((), BF16)

        def body(j, cnt):
            blk = coarse_ref[pl.ds(pl.multiple_of(j * rb, rb), rb), :]
            h = jnp.where(blk >= cf, one, nil)
            parts = [h[r * 16:(r + 1) * 16, :] for r in range(rb // 16)]
            while len(parts) > 1:
                parts = [parts[i] + parts[i + 1] for i in range(0, len(parts), 2)]
            return cnt + parts[0].astype(F32)
        cnt = lax.fori_loop(0, n_blk, body, jnp.zeros((16, tq), F32))
        return jnp.sum(cnt, axis=0, keepdims=True)

    kf = float(topk)

    def bisect(counter, first_bit, n_bits, state):
        def cond(st):
            i, _, done = st
            return jnp.logical_and(i < n_bits, jnp.min(done) < 0.5)

        def body(st):
            i, thr, done = st
            for _ in range(DSA_PASSES_PER_TEST):
                bit = jnp.where(i < n_bits, jnp.int32(1) << jnp.maximum(first_bit - i, 0), 0)
                cand = thr | bit
                n = counter(cand)
                live = done < 0.5
                thr = jnp.where(jnp.logical_and(live, n >= kf), cand, thr)
                done = jnp.where(jnp.logical_and(live, n == kf), 1.0, done)
                i = i + 1
            return i, thr, done
        _, thr, done = lax.while_loop(cond, body, (jnp.int32(0),) + state)
        return thr, done

    n_pos = count_coarse(jnp.zeros((1, tq), I32))
    thr0 = jnp.where(n_pos >= kf, 0, INT_MIN).astype(I32)
    done0 = jnp.where(n_pos == kf, 1.0, 0.0)
    thr, done = bisect(count_coarse, 30, 15, (thr0, done0))
    thr, _ = bisect(lambda cand: count(lambda blk, j: blk >= cand), 15, 16, (thr, done))

    n_gt = count(lambda blk, j: blk > thr)
    n_ge = count(lambda blk, j: blk >= thr)
    need = kf - n_gt
    idx_bits = max(1, int(math.ceil(math.log2(seq))))

    def tie_limit():
        def step(i, lim):
            cand = lim | (jnp.int32(1) << (idx_bits - 1 - i))
            below = count_hits(lambda blk, j: jnp.where(
                blk == thr, jnp.where(j * rb + rrow < cand, 1.0, 0.0), 0.0))
            return jnp.where(below < need, cand, lim)
        return lax.fori_loop(0, idx_bits, step, jnp.zeros((1, tq), I32))

    has_excess = jnp.max(n_ge - kf) > 0.0
    lim = lax.cond(has_excess, tie_limit, lambda: jnp.full((1, tq), seq, I32))

    q2 = pair_stack(q_ref[...])

    def attention(plain):
        acc_ref[...] = jnp.zeros_like(acc_ref)

        def make_bias(c):
            off = pl.multiple_of(c * kc, kc)
            key = keys_ref[pl.ds(off, kc), :]
            if plain:
                bias_ref[...] = jnp.where(key >= thr, 0.0, NEG_BIG)
            else:
                rowi = off + row0
                tie_ok = jnp.where(rowi <= lim, 0.0, NEG_BIG)
                bias = jnp.where(key > thr, 0.0, jnp.where(key == thr, tie_ok, NEG_BIG))
                bias_ref[...] = jnp.where(rowi <= t_col, bias, NEG_BIG)

        def logits(c, g):
            off = pl.multiple_of(c * kc, kc)
            kg = k_ref[pl.ds(off, kc), g * LANES:(g + 1) * LANES]
            bias = bias_ref[...]
            s2 = _dot_nt(kg, q2[g]) + jnp.concatenate([bias, bias], axis=1)
            s_ref[g] = s2
            return jnp.max(s2, axis=0, keepdims=True)

        def consume(c, g, m_old, l_old, m_chunk):
            off = pl.multiple_of(c * kc, kc)
            vg = v_ref[pl.ds(off, kc), g * LANES:(g + 1) * LANES]
            m_new = jnp.maximum(m_old, m_chunk)
            alpha = jnp.exp2(m_old - m_new)
            p = jnp.exp2(s_ref[g] - m_new)
            l_new = alpha * l_old + jnp.sum(p, axis=0, keepdims=True)
            acc_ref[g] = alpha * acc_ref[g] + _dot_tn(vg, p.astype(BF16))
            return m_new, l_new

        make_bias(0)
        mc0 = tuple(logits(0, g) for g in range(n_pairs))

        def attend(c, carry):
            ms, ls, mcs = carry
            make_bias(c)
            new_m, new_l, new_mc = [], [], []
            for g in range(n_pairs):
                m_new, l_new = consume(c - 1, g, ms[g], ls[g], mcs[g])
                new_mc.append(logits(c, g))
                new_m.append(m_new)
                new_l.append(l_new)
            return tuple(new_m), tuple(new_l), tuple(new_mc)

        init = (tuple(jnp.full((1, 2 * tq), NEG_BIG, F32) for _ in range(n_pairs)),
                tuple(jnp.zeros((1, 2 * tq), F32) for _ in range(n_pairs)), mc0)
        ms, ls, mcs = lax.fori_loop(1, n_kc, attend, init)
        ls = [consume(n_kc - 1, g, ms[g], ls[g], mcs[g])[1] for g in range(n_pairs)]

        for g in range(n_pairs):
            o2 = acc_ref[g] / ls[g]
            o_ref[:, g * LANES:(g + 1) * LANES] = jnp.where(lane_first, o2[:, :tq].T,
                                                            o2[:, tq:].T)

    plain = jnp.logical_and(jnp.logical_not(has_excess), qb * tq + 1 >= topk)
    pl.when(plain)(lambda: attention(True))
    pl.when(jnp.logical_not(plain))(lambda: attention(False))


def _dsa(qi, kw, q, ki, k, v, batch, seq):
    tq = DSA_TQ
    kc = DSA_KC if seq % DSA_KC == 0 else DSA_RB
    nqb = seq // tq
    topk = min(MAX_TOPK, seq // 4)
    qrow = lambda w: pl.BlockSpec((tq, w), lambda b, i: (b * nqb + i, 0))
    kvrow = lambda w: pl.BlockSpec((seq, w), lambda b, i: (b, 0))
    return pl.pallas_call(
        functools.partial(_dsa_kernel, seq=seq, topk=topk, kc=kc),
        grid=(batch, nqb),
        in_specs=[qrow(WIDTH), qrow(LANES), qrow(WIDTH), kvrow(LANES), kvrow(WIDTH), kvrow(WIDTH)],
        out_specs=qrow(WIDTH),
        out_shape=jax.ShapeDtypeStruct((batch * seq, WIDTH), F32),
        scratch_shapes=[pltpu.VMEM((seq, tq), I32),
                        pltpu.VMEM((WIDTH // LANES, LANES, 2 * tq), F32),
                        pltpu.VMEM((WIDTH // LANES, kc, 2 * tq), F32),
                        pltpu.VMEM((kc, tq), F32),
                        pltpu.VMEM((seq, tq), BF16)],
        compiler_params=pltpu.CompilerParams(dimension_semantics=("arbitrary", "arbitrary"),
                                             vmem_limit_bytes=VMEM_LIMIT),
        name="dsa",
    )(qi, kw, q, ki, k, v)


def _tail_kernel(x_ref, ya_ref, yb_ref, gate_ref, wor_ref, woa_ref, wout_ref, g2_ref,
                 wfg_ref, wfu_ref, wfo_ref, gf_ref, o_ref):
    gate = gate_ref[...]
    ma = _dot(ya_ref[...].astype(BF16), wor_ref[...])
    mb = _dot(yb_ref[...].astype(BF16), woa_ref[...])
    merged = gate[:, :D_MODEL] * ma + gate[:, D_MODEL:] * mb
    h = x_ref[...] + _dot(merged.astype(BF16), wout_ref[...])
    hn = h * lax.rsqrt(jnp.mean(h * h, axis=-1, keepdims=True) + NORM_EPS) * g2_ref[...]
    hb = hn.astype(BF16)
    ffn = jnp.zeros_like(h)
    half = D_FF // 2
    for f in range(2):
        sl = slice(f * half, (f + 1) * half)
        zg = _dot(hb, wfg_ref[:, sl])
        zu = _dot(hb, wfu_ref[:, sl])
        act = zg * _sigmoid(zg) * zu
        ffn = ffn + _dot(act.astype(BF16), wfo_ref[sl, :])
    h = h + ffn
    o_ref[...] = h * lax.rsqrt(jnp.mean(h * h, axis=-1, keepdims=True) + NORM_EPS) * gf_ref[...]


def _tail(x2, ya, yb, gate, w_o_rwkv, w_o_att, w_out, norm2_g, w_ffn_in, w_ffn_out, normf_g):
    tokens = x2.shape[0]
    tm = 512 if tokens % 512 == 0 else tokens
    row = lambda w: pl.BlockSpec((tm, w), lambda i: (i, 0))
    const = lambda shape: pl.BlockSpec(shape, lambda i: (0, 0), pipeline_mode=pl.Buffered(1))
    return pl.pallas_call(
        _tail_kernel,
        grid=(tokens // tm,),
        in_specs=[row(D_MODEL), row(WIDTH), row(WIDTH), row(GATE_COLS),
                  const((WIDTH, D_MODEL)), const((WIDTH, D_MODEL)), const((D_MODEL, D_MODEL)),
                  const((1, D_MODEL)), const((D_MODEL, D_FF)), const((D_MODEL, D_FF)),
                  const((D_FF, D_MODEL)), const((1, D_MODEL))],
        out_specs=row(D_MODEL),
        out_shape=jax.ShapeDtypeStruct((tokens, D_MODEL), F32),
        compiler_params=pltpu.CompilerParams(dimension_semantics=("arbitrary",),
                                             vmem_limit_bytes=VMEM_LIMIT),
        name="tail",
    )(x2, ya, yb, gate, w_o_rwkv.astype(BF16), w_o_att.astype(BF16), w_out.astype(BF16),
      norm2_g.reshape(1, -1), w_ffn_in[:, :D_FF].astype(BF16), w_ffn_in[:, D_FF:].astype(BF16),
      w_ffn_out.astype(BF16), normf_g.reshape(1, -1))


def kernel(x, norm1_g, w_in, tshift_mu, w_decay_up, w0, a_up, a0, g_up, k_k, k_a, r_k, lnx_g, lnx_b, w_o_rwkv, w_o_att, w_out, norm2_g, w_ffn_in, w_ffn_out, normf_g):
    batch, seq, _ = x.shape
    x2 = x.reshape(batch * seq, D_MODEL)
    ps, q, k, v, qi, ki, kw, gate = _project(x2, norm1_g, w_in, tshift_mu, seq)
    ya = _rwkv(ps, w_decay_up, w0, a_up, a0, g_up, k_k, k_a, r_k.reshape(-1), lnx_g, lnx_b,
               batch, seq)
    yb = _dsa(qi, kw, q, ki, k, v, batch, seq)
    out = _tail(x2, ya, yb, gate, w_o_rwkv, w_o_att, w_out, norm2_g, w_ffn_in, w_ffn_out, normf_g)
    return out.reshape(batch, seq, D_MODEL)
```

```python
import functools
import math

import jax
import jax.numpy as jnp
import numpy as np
from jax import lax
from jax.experimental import pallas as pl
from jax.experimental.pallas import tpu as pltpu

F32 = jnp.float32
BF16 = jnp.bfloat16
I32 = jnp.int32

D_MODEL = 1024
HEAD_DIM = 64
N_HEADS = 8
WIDTH = N_HEADS * HEAD_DIM
DECAY_RANK = 64
AAA_RANK = 64
GATE_RANK = 128
MAX_TOPK = 256
ROPE_THETA = 10000.0
D_FF = 2816
NORM_EPS = 1e-6
LNX_EPS = 64e-5
RWKV_COLS = 3 * WIDTH + DECAY_RANK + AAA_RANK + GATE_RANK
ATT_COLS = 3 * WIDTH
IDX_COLS = N_HEADS * HEAD_DIM + HEAD_DIM + N_HEADS
IDX_PAD = 640
GATE_COLS = 2 * D_MODEL
LANES = 128
VMEM_LIMIT = 56 * 1024 * 1024

INT_MIN = -(2 ** 31)
LOG2_E = 1.4426950408889634
NEG_BIG = -1e30


def _dot(a, b, precision=None):
    return jnp.dot(a, b, preferred_element_type=F32, precision=precision)


def _dot_nt(a, b, precision=None):
    return lax.dot_general(a, b, (((1,), (1,)), ((), ())), preferred_element_type=F32,
                           precision=precision)


def _dot_tn(a, b, precision=None):
    return lax.dot_general(a, b, (((0,), (0,)), ((), ())), preferred_element_type=F32,
                           precision=precision)


def _sigmoid(x):
    return 1.0 / (1.0 + jnp.exp(-x))


def _rope_group(xg, cos, sin_signed, lane_lo):
    partner = jnp.where(lane_lo, pltpu.roll(xg, LANES - HEAD_DIM // 2, 1),
                        pltpu.roll(xg, HEAD_DIM // 2, 1))
    return xg * cos + partner * sin_signed


def _proj_kernel(x_ref, g1_ref, wr_ref, wa_ref, wi_ref, wg_ref, mu_ref, cos_ref, sin_ref,
                 ps_ref, q_ref, k_ref, v_ref, qi_ref, ki_ref, kw_ref, gate_ref, carry_ref,
                 *, tiles_per_seq):
    i = pl.program_id(0)
    x = x_ref[...]
    tm = x.shape[0]
    u = x * lax.rsqrt(jnp.mean(x * x, axis=-1, keepdims=True) + NORM_EPS) * g1_ref[...]
    ub = u.astype(BF16)

    pr = _dot(ub, wr_ref[...])

    @pl.when(i % tiles_per_seq == 0)
    def _():
        carry_ref[...] = jnp.zeros_like(carry_ref)

    row = lax.broadcasted_iota(I32, (tm, 1), 0)
    prev = jnp.where(row == 0, carry_ref[7:8, :], pltpu.roll(pr, 1, 0))
    ps_ref[...] = pr + (prev - pr) * mu_ref[...]
    carry_ref[...] = pr[tm - 8:tm, :]

    cos = cos_ref[...]
    sin_signed = sin_ref[...]
    lane = lax.broadcasted_iota(I32, (1, LANES), 1)
    lane_lo = (lane % HEAD_DIM) < HEAD_DIM // 2

    pa = _dot(ub, wa_ref[...])
    for g in range(WIDTH // LANES):
        sl = slice(g * LANES, (g + 1) * LANES)
        qg = _rope_group(pa[:, sl], cos, sin_signed, lane_lo) * (HEAD_DIM ** -0.5 * LOG2_E)
        q_ref[:, sl] = qg.astype(BF16)
        kg = _rope_group(pa[:, WIDTH + g * LANES:WIDTH + (g + 1) * LANES], cos, sin_signed, lane_lo)
        k_ref[:, sl] = kg.astype(BF16)
    v_ref[...] = pa[:, 2 * WIDTH:].astype(BF16)

    pi = _dot(ub, wi_ref[...])
    for g in range(WIDTH // LANES):
        sl = slice(g * LANES, (g + 1) * LANES)
        qi_ref[:, sl] = _rope_group(pi[:, sl], cos, sin_signed, lane_lo).astype(BF16)
    tail = pi[:, WIDTH:WIDTH + LANES]
    kw_ref[...] = tail
    kr = _rope_group(tail, cos, sin_signed, lane_lo)
    ki_ref[...] = jnp.where(lane < HEAD_DIM, kr, pltpu.roll(kr, HEAD_DIM, 1)).astype(BF16)

    gate_ref[...] = _sigmoid(_dot(ub, wg_ref[...]))


def _rope_tables(seq):
    half = HEAD_DIM // 2
    inv = 1.0 / (ROPE_THETA ** (jnp.arange(half, dtype=F32) * 2.0 / HEAD_DIM))
    ang = jnp.arange(seq, dtype=F32)[:, None] * inv[None, :]
    cos, sin = jnp.cos(ang), jnp.sin(ang)
    cos_t = jnp.concatenate([cos, cos, cos, cos], axis=1)
    sin_t = jnp.concatenate([-sin, sin, -sin, sin], axis=1)
    return cos_t, sin_t


def _full(shape):
    return pl.BlockSpec(shape, lambda *_: (0,) * len(shape))


def _project(x2, norm1_g, w_in, tshift_mu, seq):
    tokens = x2.shape[0]
    tm = 512 if seq % 512 == 0 else seq
    c0, c1, c2 = RWKV_COLS, RWKV_COLS + ATT_COLS, RWKV_COLS + ATT_COLS + IDX_COLS
    wr = w_in[:, :c0].astype(BF16)
    wa = w_in[:, c0:c1].astype(BF16)
    wi = jnp.pad(w_in[:, c1:c2], ((0, 0), (0, IDX_PAD - IDX_COLS))).astype(BF16)
    wg = w_in[:, c2:].astype(BF16)
    cos_t, sin_t = _rope_tables(seq)
    tps = seq // tm
    row = lambda w: pl.BlockSpec((tm, w), lambda i: (i, 0))
    pos = pl.BlockSpec((tm, LANES), lambda i: (i % tps, 0))
    out_shape = (
        jax.ShapeDtypeStruct((tokens, RWKV_COLS), F32),
        jax.ShapeDtypeStruct((tokens, WIDTH), BF16),
        jax.ShapeDtypeStruct((tokens, WIDTH), BF16),
        jax.ShapeDtypeStruct((tokens, WIDTH), BF16),
        jax.ShapeDtypeStruct((tokens, WIDTH), BF16),
        jax.ShapeDtypeStruct((tokens, LANES), BF16),
        jax.ShapeDtypeStruct((tokens, LANES), F32),
        jax.ShapeDtypeStruct((tokens, GATE_COLS), F32),
    )
    return pl.pallas_call(
        functools.partial(_proj_kernel, tiles_per_seq=tps),
        grid=(tokens // tm,),
        in_specs=[row(D_MODEL), _full((1, D_MODEL)), _full((D_MODEL, RWKV_COLS)),
                  _full((D_MODEL, ATT_COLS)), _full((D_MODEL, IDX_PAD)), _full((D_MODEL, GATE_COLS)),
                  _full((1, RWKV_COLS)), pos, pos],
        out_specs=(row(RWKV_COLS), row(WIDTH), row(WIDTH), row(WIDTH), row(WIDTH),
                   row(LANES), row(LANES), row(GATE_COLS)),
        out_shape=out_shape,
        scratch_shapes=[pltpu.VMEM((8, RWKV_COLS), F32)],
        compiler_params=pltpu.CompilerParams(dimension_semantics=("arbitrary",),
                                             vmem_limit_bytes=VMEM_LIMIT),
        name="proj",
    )(x2, norm1_g.reshape(1, -1), wr, wa, wi, wg, tshift_mu.reshape(1, -1), cos_t, sin_t)


RWKV_CHUNK = 64
RWKV_CHUNKS_PER_STEP = 2
HI = lax.Precision.HIGHEST


def _pair_sum(x, lane_first):
    s0 = jnp.sum(jnp.where(lane_first, x, 0.0), axis=1, keepdims=True)
    s1 = jnp.sum(jnp.where(lane_first, 0.0, x), axis=1, keepdims=True)
    return jnp.where(lane_first, s0, s1)


def _rwkv_kernel(ps_ref, wdu_ref, w0_ref, aup_ref, a0_ref, gup_ref, kk_ref, ka_ref, rk_ref,
                 lng_ref, lnb_ref, y_ref, state_ref):
    c = pl.program_id(1)
    L = RWKV_CHUNK

    @pl.when(c == 0)
    def _():
        state_ref[...] = jnp.zeros_like(state_ref)

    ps = ps_ref[...]
    r_all = ps[:, :WIDTH]
    k_all = ps[:, WIDTH:2 * WIDTH]
    v_all = ps[:, 2 * WIDTH:3 * WIDTH]
    o = 3 * WIDTH
    wd = ps[:, o:o + DECAY_RANK]
    ad = ps[:, o + DECAY_RANK:o + DECAY_RANK + AAA_RANK]
    gd = ps[:, o + DECAY_RANK + AAA_RANK:]

    bf = lambda t: t.astype(BF16)
    z = -(w0_ref[...] + _dot(bf(jnp.tanh(wd)), wdu_ref[...]))
    softplus = jnp.maximum(z, 0.0) + jnp.log(1.0 + jnp.exp(-jnp.abs(z)))
    w_all = -softplus - 0.5
    ld_all = -jnp.exp(w_all)
    a_all = _sigmoid(a0_ref[...] + _dot(bf(ad), aup_ref[...]))
    g_all = _dot(bf(_sigmoid(gd)), gup_ref[...])

    lane = lax.broadcasted_iota(I32, (1, LANES), 1)
    lane_first = lane < HEAD_DIM
    ti = lax.broadcasted_iota(I32, (L, 2 * L), 0)
    tj = lax.broadcasted_iota(I32, (L, 2 * L), 1) % L
    strict = ti > tj
    incl = ti >= tj
    bi = lax.broadcasted_iota(I32, (LANES, 2 * LANES), 0)
    bj = lax.broadcasted_iota(I32, (LANES, 2 * LANES), 1) % LANES
    same_head = (bi < HEAD_DIM) == (bj < HEAD_DIM)
    eye = (bi == bj)[:, :LANES]
    tri_ones = jnp.where(incl[:, :L], 1.0, 0.0)
    zeros_l = jnp.zeros((L, LANES), BF16)

    n_pairs = WIDTH // LANES
    lane2_first = jnp.concatenate([lane_first, lane_first], axis=1)
    pr = {}
    for ci in range(RWKV_CHUNKS_PER_STEP):
        rows = slice(ci * L, (ci + 1) * L)
        for g in range(n_pairs):
            sl = slice(g * LANES, (g + 1) * LANES)
            r, k, v = r_all[rows, sl], k_all[rows, sl], v_all[rows, sl]
            a, ld = a_all[rows, sl], ld_all[rows, sl]
            kk = k * kk_ref[:, sl]
            nrm = jnp.sqrt(_pair_sum(kk * kk, lane_first))
            kk = kk / jnp.maximum(nrm, 1e-12)
            k = k * (1.0 + (a - 1.0) * ka_ref[:, sl])
            cum = _dot(tri_ones, ld, HI)
            p_inc = jnp.exp(cum)
            p_inv = jnp.exp(-cum)
            al = -kk * jnp.exp(cum - ld)
            be = kk * a * p_inv
            kt = k * p_inv
            rt = r * p_inc
            p_last = p_inc[L - 1:L, :]
            v_b = bf(v)
            pr[ci, g] = dict(
                r=r, k=k, v=v, al=al, rt=rt, p_last=p_last,
                bk=bf(jnp.concatenate([be, kt], axis=0)),
                bkp=bf(jnp.concatenate([be * p_last, kt * p_last], axis=0)),
                zv=jnp.concatenate([zeros_l, v_b], axis=0),
                zv2=jnp.concatenate([zeros_l, v_b], axis=1))

    heads = [(ci, g, hh) for ci in range(RWKV_CHUNKS_PER_STEP) for g in range(n_pairs)
             for hh in range(2)]
    gram = []
    for ci, g, hh in heads:
        mask = lane_first if hh == 0 else jnp.logical_not(lane_first)
        ar = bf(jnp.concatenate([jnp.where(mask, pr[ci, g]["al"], 0.0),
                                 jnp.where(mask, pr[ci, g]["rt"], 0.0)], axis=0))
        gram.append(_dot_nt(ar, pr[ci, g]["bk"]))
    top = [jnp.where(strict, gm[:L], 0.0) for gm in gram]
    bot = [bf(jnp.where(incl, gm[L:], 0.0)) for gm in gram]
    rhs = [jnp.concatenate([pr[ci, g]["al"], _dot(bf(top[i]), pr[ci, g]["zv"])], axis=1)
           for i, (ci, g, hh) in enumerate(heads)]
    npow = [bf(t[:, :L]) for t in top]
    u = [rhs[i] + _dot(npow[i], bf(rhs[i])) for i in range(len(heads))]
    for _ in range(int(math.log2(L)) - 1):
        npow = [bf(_dot(n, n)) for n in npow]
        u = [u[i] + _dot(npow[i], bf(u[i])) for i in range(len(heads))]
    u_b = [bf(x) for x in u]
    o_h = [_dot(bot[i], jnp.concatenate([u_b[i], pr[ci, g]["zv2"]], axis=0))
           for i, (ci, g, hh) in enumerate(heads)]

    for g in range(n_pairs):
        sl = slice(g * LANES, (g + 1) * LANES)
        st = state_ref[g]
        for ci in range(RWKV_CHUNKS_PER_STEP):
            rows = slice(ci * L, (ci + 1) * L)
            p = pr[ci, g]
            i0 = (ci * n_pairs + g) * 2
            ub = jnp.where(lane2_first, u_b[i0], u_b[i0 + 1])
            o2 = jnp.where(lane2_first, o_h[i0], o_h[i0 + 1])
            ry = p["rt"] + o2[:, :LANES]
            yl = o2[:, LANES:]
            uz = jnp.concatenate([ub, p["zv2"]], axis=0)
            mg = jnp.where(same_head, _dot_tn(p["bkp"], uz), 0.0)
            m_mat = jnp.where(eye, p["p_last"], 0.0) + mg[:, :LANES]
            g_mat = mg[:, LANES:]

            st_b = bf(st)
            y = _dot(bf(ry), st_b) + yl
            st = _dot(bf(m_mat), st_b) + g_mat

            mean = _pair_sum(y, lane_first) * (1.0 / HEAD_DIM)
            yc = y - mean
            var = _pair_sum(yc * yc, lane_first) * (1.0 / HEAD_DIM)
            yn = yc * lax.rsqrt(var + LNX_EPS) * lng_ref[:, sl] + lnb_ref[:, sl]
            bonus = _pair_sum(p["r"] * p["k"] * rk_ref[:, sl], lane_first) * p["v"]
            y_ref[rows, sl] = (yn + bonus) * g_all[rows, sl]
        state_ref[g] = st


def _rwkv(ps, w_decay_up, w0, a_up, a0, g_up, k_k, k_a, r_k, lnx_g, lnx_b, batch, seq):
    L = RWKV_CHUNK * RWKV_CHUNKS_PER_STEP
    n_chunks = seq // L
    vec = lambda t: t.reshape(1, WIDTH)
    return pl.pallas_call(
        _rwkv_kernel,
        grid=(batch, n_chunks),
        in_specs=[pl.BlockSpec((L, RWKV_COLS), lambda b, c: (b * n_chunks + c, 0)),
                  _full((DECAY_RANK, WIDTH)), _full((1, WIDTH)), _full((AAA_RANK, WIDTH)),
                  _full((1, WIDTH)), _full((GATE_RANK, WIDTH)), _full((1, WIDTH)),
                  _full((1, WIDTH)), _full((1, WIDTH)), _full((1, WIDTH)), _full((1, WIDTH))],
        out_specs=pl.BlockSpec((L, WIDTH), lambda b, c: (b * n_chunks + c, 0)),
        out_shape=jax.ShapeDtypeStruct((batch * seq, WIDTH), F32),
        scratch_shapes=[pltpu.VMEM((WIDTH // LANES, LANES, LANES), F32)],
        compiler_params=pltpu.CompilerParams(dimension_semantics=("arbitrary", "arbitrary"),
                                             vmem_limit_bytes=VMEM_LIMIT),
        name="rwkv",
    )(ps, w_decay_up.astype(BF16), vec(w0), a_up.astype(BF16), vec(a0), g_up.astype(BF16),
      vec(k_k), vec(k_a), vec(r_k), vec(lnx_g), vec(lnx_b))


DSA_TQ = LANES
DSA_KC = 512
DSA_RB = 512
DSA_PASSES_PER_TEST = 4


def _dsa_kernel(qi_ref, kw_ref, q_ref, ki_ref, k_ref, v_ref, o_ref, keys_ref, acc_ref,
                s_ref, bias_ref, *, seq, topk, kc):
    tq = DSA_TQ
    qb = pl.program_id(1)
    n_kc = (qb * tq + tq + kc - 1) // kc
    t_col = qb * tq + lax.broadcasted_iota(I32, (1, tq), 1)
    row0 = lax.broadcasted_iota(I32, (kc, 1), 0)
    lane = lax.broadcasted_iota(I32, (1, LANES), 1)
    lane_first = lane < HEAD_DIM
    idx_scale = (HEAD_DIM ** -0.5) * (N_HEADS ** -0.5)
    zero = jnp.zeros((), BF16)
    n_pairs = WIDTH // LANES

    def pair_stack(x):
        out = []
        for g in range(WIDTH // LANES):
            xg = x[:, g * LANES:(g + 1) * LANES]
            out.append(jnp.concatenate([jnp.where(lane_first, xg, zero),
                                        jnp.where(lane_first, zero, xg)], axis=0))
        return out

    qi2 = pair_stack(qi_ref[...])
    kw_t = kw_ref[...].T
    w_h = [kw_t[HEAD_DIM + h:HEAD_DIM + h + 1, :] * idx_scale for h in range(N_HEADS)]

    def stage_scores(c, g):
        off = pl.multiple_of(c * kc, kc)
        s_ref[g] = _dot_nt(ki_ref[pl.ds(off, kc), :], qi2[g])

    def fold_scores(g, acc):
        s2 = s_ref[g]
        acc = acc + jnp.maximum(s2[:, :tq], 0.0) * w_h[2 * g]
        return acc + jnp.maximum(s2[:, tq:], 0.0) * w_h[2 * g + 1]

    def store_keys(c, acc):
        off = pl.multiple_of(c * kc, kc)
        score = jnp.where(acc == 0.0, 0.0, acc)
        keys_ref[pl.ds(off, kc), :] = jnp.where(off + row0 <= t_col, score, -jnp.inf)

    for g in range(n_pairs):
        stage_scores(0, g)

    def score_step(c, carry):
        acc = jnp.zeros((kc, tq), F32)
        for g in range(n_pairs):
            acc = fold_scores(g, acc)
            stage_scores(c, g)
        store_keys(c - 1, acc)
        return carry

    lax.fori_loop(1, n_kc, score_step, 0)
    acc = jnp.zeros((kc, tq), F32)
    for g in range(n_pairs):
        acc = fold_scores(g, acc)
    store_keys(n_kc - 1, acc)

    rb = DSA_RB
    n_blk = n_kc * (kc // rb)
    rrow = lax.broadcasted_iota(I32, (rb, 1), 0)

    def scan(fn, init):
        def body(j, carry):
            return fn(keys_ref[pl.ds(pl.multiple_of(j * rb, rb), rb), :], j, carry)
        return lax.fori_loop(0, n_blk, body, init)

    def tree(x, op):
        parts = [x[r * 8:(r + 1) * 8, :] for r in range(rb // 8)]
        while len(parts) > 1:
            parts = [op(parts[i], parts[i + 1]) for i in range(0, len(parts), 2)]
        return parts[0]

    def count_hits(hits):
        cnt = scan(lambda blk, j, cnt: cnt + tree(hits(blk, j), jnp.add), jnp.zeros((8, tq), F32))
        return jnp.sum(cnt, axis=0, keepdims=True)

    def count_ge(cand):
        return count_hits(lambda blk, j: jnp.where(blk >= cand, 1.0, 0.0))

    kf = float(topk)
    tiny = float(np.finfo(np.float32).tiny)
    n_valid = (t_col + 1).astype(F32)

    def first_pass(blk, j, carry):
        cnt, mx = carry
        return (cnt + tree(jnp.where(blk >= 0.0, 1.0, 0.0), jnp.add), jnp.maximum(mx, tree(blk, jnp.maximum)))

    cnt0, mx = scan(first_pass, (jnp.zeros((8, tq), F32), jnp.full((8, tq), -jnp.inf, F32)))
    n_pos = jnp.sum(cnt0, axis=0, keepdims=True)
    top = jnp.max(mx, axis=0, keepdims=True)
    tbits = pltpu.bitcast(top, I32)
    above = pltpu.bitcast(jnp.where(top > 0.0, tbits + 1, tbits - 1), F32)
    above = jnp.where(top == 0.0, tiny, above)
    nonneg = n_pos >= kf

    def lowest():
        def body(blk, j, mn):
            return jnp.minimum(mn, tree(jnp.where(blk == -jnp.inf, jnp.inf, blk), jnp.minimum))
        mn = scan(body, jnp.full((8, tq), jnp.inf, F32))
        return jnp.min(mn, axis=0, keepdims=True)

    need_low = jnp.max(jnp.where(jnp.logical_and(jnp.logical_not(nonneg), n_valid > kf), 1.0, 0.0)) > 0.0
    low = lax.cond(need_low, lowest, lambda: jnp.zeros((1, tq), F32))

    short = n_valid <= kf
    lo0 = jnp.where(short, -jnp.inf, jnp.where(nonneg, 0.0, low))
    hi0 = jnp.where(nonneg, above, 0.0)
    c_lo0 = jnp.where(short, n_valid, jnp.where(nonneg, n_pos, n_valid))
    done0 = jnp.where(jnp.logical_or(short, c_lo0 == kf), 1.0, 0.0)

    def bisect_cond(st):
        i, lo, hi, c_lo, done = st
        return jnp.min(done) < 0.5

    def bisect_body(st):
        i, lo, hi, c_lo, done = st
        for _ in range(DSA_PASSES_PER_TEST):
            mid = lo * 0.5 + hi * 0.5
            mid = jnp.where(i == 0, jnp.where(lo == 0.0, tiny, jnp.where(hi == 0.0, -tiny, mid)), mid)
            stuck = jnp.logical_or(mid <= lo, mid >= hi)
            n = count_ge(mid)
            live = jnp.logical_and(done < 0.5, jnp.logical_not(stuck))
            up = jnp.logical_and(live, n >= kf)
            lo = jnp.where(up, mid, lo)
            c_lo = jnp.where(up, n, c_lo)
            hi = jnp.where(jnp.logical_and(live, n < kf), mid, hi)
            done = jnp.where(jnp.logical_or(stuck, jnp.logical_and(live, n == kf)), 1.0, done)
            i = i + 1
        return i, lo, hi, c_lo, done

    _, thr, _, n_ge, _ = lax.while_loop(bisect_cond, bisect_body,
                                        (jnp.int32(0), lo0, hi0, c_lo0, done0))

    idx_bits = max(1, int(math.ceil(math.log2(seq))))

    def tie_limit():
        need = kf - count_hits(lambda blk, j: jnp.where(blk > thr, 1.0, 0.0))

        def step(i, lim):
            cand = lim | (jnp.int32(1) << (idx_bits - 1 - i))
            below = count_hits(lambda blk, j: jnp.where(
                blk == thr, jnp.where(j * rb + rrow < cand, 1.0, 0.0), 0.0))
            return jnp.where(below < need, cand, lim)
        return lax.fori_loop(0, idx_bits, step, jnp.zeros((1, tq), I32))

    has_excess = jnp.max(n_ge - kf) > 0.0
    lim = lax.cond(has_excess, tie_limit, lambda: jnp.full((1, tq), seq, I32))

    q2 = pair_stack(q_ref[...])

    def attention(plain):
        acc_ref[...] = jnp.zeros_like(acc_ref)

        def make_bias(c):
            off = pl.multiple_of(c * kc, kc)
            key = keys_ref[pl.ds(off, kc), :]
            if plain:
                bias_ref[...] = jnp.where(key >= thr, 0.0, NEG_BIG)
            else:
                rowi = off + row0
                tie_ok = jnp.where(rowi <= lim, 0.0, NEG_BIG)
                bias = jnp.where(key > thr, 0.0, jnp.where(key == thr, tie_ok, NEG_BIG))
                bias_ref[...] = jnp.where(rowi <= t_col, bias, NEG_BIG)

        def logits(c, g):
            off = pl.multiple_of(c * kc, kc)
            kg = k_ref[pl.ds(off, kc), g * LANES:(g + 1) * LANES]
            bias = bias_ref[...]
            s2 = _dot_nt(kg, q2[g]) + jnp.concatenate([bias, bias], axis=1)
            s_ref[g] = s2
            return jnp.max(s2, axis=0, keepdims=True)

        def consume(c, g, m_old, l_old, m_chunk):
            off = pl.multiple_of(c * kc, kc)
            vg = v_ref[pl.ds(off, kc), g * LANES:(g + 1) * LANES]
            m_new = jnp.maximum(m_old, m_chunk)
            alpha = jnp.exp2(m_old - m_new)
            p = jnp.exp2(s_ref[g] - m_new)
            l_new = alpha * l_old + jnp.sum(p, axis=0, keepdims=True)
            acc_ref[g] = alpha * acc_ref[g] + _dot_tn(vg, p.astype(BF16))
            return m_new, l_new

        make_bias(0)
        mc0 = tuple(logits(0, g) for g in range(n_pairs))

        def attend(c, carry):
            ms, ls, mcs = carry
            make_bias(c)
            new_m, new_l, new_mc = [], [], []
            for g in range(n_pairs):
                m_new, l_new = consume(c - 1, g, ms[g], ls[g], mcs[g])
                new_mc.append(logits(c, g))
                new_m.append(m_new)
                new_l.append(l_new)
            return tuple(new_m), tuple(new_l), tuple(new_mc)

        init = (tuple(jnp.full((1, 2 * tq), NEG_BIG, F32) for _ in range(n_pairs)),
                tuple(jnp.zeros((1, 2 * tq), F32) for _ in range(n_pairs)), mc0)
        ms, ls, mcs = lax.fori_loop(1, n_kc, attend, init)
        ls = [consume(n_kc - 1, g, ms[g], ls[g], mcs[g])[1] for g in range(n_pairs)]

        for g in range(n_pairs):
            o2 = acc_ref[g] / ls[g]
            o_ref[:, g * LANES:(g + 1) * LANES] = jnp.where(lane_first, o2[:, :tq].T,
                                                            o2[:, tq:].T)

    plain = jnp.logical_and(jnp.logical_not(has_excess), qb * tq + 1 >= topk)
    pl.when(plain)(lambda: attention(True))
    pl.when(jnp.logical_not(plain))(lambda: attention(False))


def _dsa(qi, kw, q, ki, k, v, batch, seq):
    tq = DSA_TQ
    kc = DSA_KC if seq % DSA_KC == 0 else DSA_RB
    nqb = seq // tq
    topk = min(MAX_TOPK, seq // 4)
    qrow = lambda w: pl.BlockSpec((tq, w), lambda b, i: (b * nqb + i, 0))
    kvrow = lambda w: pl.BlockSpec((seq, w), lambda b, i: (b, 0))
    return pl.pallas_call(
        functools.partial(_dsa_kernel, seq=seq, topk=topk, kc=kc),
        grid=(batch, nqb),
        in_specs=[qrow(WIDTH), qrow(LANES), qrow(WIDTH), kvrow(LANES), kvrow(WIDTH), kvrow(WIDTH)],
        out_specs=qrow(WIDTH),
        out_shape=jax.ShapeDtypeStruct((batch * seq, WIDTH), F32),
        scratch_shapes=[pltpu.VMEM((seq, tq), F32),
                        pltpu.VMEM((WIDTH // LANES, LANES, 2 * tq), F32),
                        pltpu.VMEM((WIDTH // LANES, kc, 2 * tq), F32),
                        pltpu.VMEM((kc, tq), F32)],
        compiler_params=pltpu.CompilerParams(dimension_semantics=("arbitrary", "arbitrary"),
                                             vmem_limit_bytes=VMEM_LIMIT),
        name="dsa",
    )(qi, kw, q, ki, k, v)


def _tail_kernel(x_ref, ya_ref, yb_ref, gate_ref, wor_ref, woa_ref, wout_ref, g2_ref,
                 wfg_ref, wfu_ref, wfo_ref, gf_ref, o_ref):
    gate = gate_ref[...]
    ma = _dot(ya_ref[...].astype(BF16), wor_ref[...])
    mb = _dot(yb_ref[...].astype(BF16), woa_ref[...])
    merged = gate[:, :D_MODEL] * ma + gate[:, D_MODEL:] * mb
    h = x_ref[...] + _dot(merged.astype(BF16), wout_ref[...])
    hn = h * lax.rsqrt(jnp.mean(h * h, axis=-1, keepdims=True) + NORM_EPS) * g2_ref[...]
    hb = hn.astype(BF16)
    ffn = jnp.zeros_like(h)
    half = D_FF // 2
    for f in range(2):
        sl = slice(f * half, (f + 1) * half)
        zg = _dot(hb, wfg_ref[:, sl])
        zu = _dot(hb, wfu_ref[:, sl])
        act = zg * _sigmoid(zg) * zu
        ffn = ffn + _dot(act.astype(BF16), wfo_ref[sl, :])
    h = h + ffn
    o_ref[...] = h * lax.rsqrt(jnp.mean(h * h, axis=-1, keepdims=True) + NORM_EPS) * gf_ref[...]


def _tail(x2, ya, yb, gate, w_o_rwkv, w_o_att, w_out, norm2_g, w_ffn_in, w_ffn_out, normf_g):
    tokens = x2.shape[0]
    tm = 512 if tokens % 512 == 0 else tokens
    row = lambda w: pl.BlockSpec((tm, w), lambda i: (i, 0))
    const = lambda shape: pl.BlockSpec(shape, lambda i: (0, 0), pipeline_mode=pl.Buffered(1))
    return pl.pallas_call(
        _tail_kernel,
        grid=(tokens // tm,),
        in_specs=[row(D_MODEL), row(WIDTH), row(WIDTH), row(GATE_COLS),
                  const((WIDTH, D_MODEL)), const((WIDTH, D_MODEL)), const((D_MODEL, D_MODEL)),
                  const((1, D_MODEL)), const((D_MODEL, D_FF)), const((D_MODEL, D_FF)),
                  const((D_FF, D_MODEL)), const((1, D_MODEL))],
        out_specs=row(D_MODEL),
        out_shape=jax.ShapeDtypeStruct((tokens, D_MODEL), F32),
        compiler_params=pltpu.CompilerParams(dimension_semantics=("arbitrary",),
                                             vmem_limit_bytes=VMEM_LIMIT),
        name="tail",
    )(x2, ya, yb, gate, w_o_rwkv.astype(BF16), w_o_att.astype(BF16), w_out.astype(BF16),
      norm2_g.reshape(1, -1), w_ffn_in[:, :D_FF].astype(BF16), w_ffn_in[:, D_FF:].astype(BF16),
      w_ffn_out.astype(BF16), normf_g.reshape(1, -1))


def kernel(x, norm1_g, w_in, tshift_mu, w_decay_up, w0, a_up, a0, g_up, k_k, k_a, r_k, lnx_g, lnx_b, w_o_rwkv, w_o_att, w_out, norm2_g, w_ffn_in, w_ffn_out, normf_g):
    batch, seq, _ = x.shape
    x2 = x.reshape(batch * seq, D_MODEL)
    ps, q, k, v, qi, ki, kw, gate = _project(x2, norm1_g, w_in, tshift_mu, seq)
    ya = _rwkv(ps, w_decay_up, w0, a_up, a0, g_up, k_k, k_a, r_k.reshape(-1), lnx_g, lnx_b,
               batch, seq)
    yb = _dsa(qi, kw, q, ki, k, v, batch, seq)
    out = _tail(x2, ya, yb, gate, w_o_rwkv, w_o_att, w_out, norm2_g, w_ffn_in, w_ffn_out, normf_g)
    return out.reshape(batch, seq, D_MODEL)
```

```python
import functools
import math

import jax
import jax.numpy as jnp
import numpy as np
from jax import lax
from jax.experimental import pallas as pl
from jax.experimental.pallas import tpu as pltpu

F32 = jnp.float32
BF16 = jnp.bfloat16
I32 = jnp.int32

D_MODEL = 1024
HEAD_DIM = 64
N_HEADS = 8
WIDTH = N_HEADS * HEAD_DIM
DECAY_RANK = 64
AAA_RANK = 64
GATE_RANK = 128
MAX_TOPK = 256
ROPE_THETA = 10000.0
D_FF = 2816
NORM_EPS = 1e-6
LNX_EPS = 64e-5
RWKV_COLS = 3 * WIDTH + DECAY_RANK + AAA_RANK + GATE_RANK
ATT_COLS = 3 * WIDTH
IDX_COLS = N_HEADS * HEAD_DIM + HEAD_DIM + N_HEADS
IDX_PAD = 640
GATE_COLS = 2 * D_MODEL
LANES = 128
VMEM_LIMIT = 56 * 1024 * 1024

INT_MIN = -(2 ** 31)
LOG2_E = 1.4426950408889634
NEG_BIG = -1e30


def _dot(a, b, precision=None):
    return jnp.dot(a, b, preferred_element_type=F32, precision=precision)


def _dot_nt(a, b, precision=None):
    return lax.dot_general(a, b, (((1,), (1,)), ((), ())), preferred_element_type=F32,
                           precision=precision)


def _dot_tn(a, b, precision=None):
    return lax.dot_general(a, b, (((0,), (0,)), ((), ())), preferred_element_type=F32,
                           precision=precision)


def _sigmoid(x):
    return 1.0 / (1.0 + jnp.exp(-x))


def _rope_group(xg, cos, sin_signed, lane_lo):
    partner = jnp.where(lane_lo, pltpu.roll(xg, LANES - HEAD_DIM // 2, 1),
                        pltpu.roll(xg, HEAD_DIM // 2, 1))
    return xg * cos + partner * sin_signed


def _proj_kernel(x_ref, g1_ref, wr_ref, wa_ref, wi_ref, wg_ref, mu_ref, cos_ref, sin_ref,
                 ps_ref, q_ref, k_ref, v_ref, qi_ref, ki_ref, kw_ref, gate_ref, carry_ref,
                 *, tiles_per_seq):
    i = pl.program_id(0)
    x = x_ref[...]
    tm = x.shape[0]
    u = x * lax.rsqrt(jnp.mean(x * x, axis=-1, keepdims=True) + NORM_EPS) * g1_ref[...]
    ub = u.astype(BF16)

    pr = _dot(ub, wr_ref[...])

    @pl.when(i % tiles_per_seq == 0)
    def _():
        carry_ref[...] = jnp.zeros_like(carry_ref)

    row = lax.broadcasted_iota(I32, (tm, 1), 0)
    prev = jnp.where(row == 0, carry_ref[7:8, :], pltpu.roll(pr, 1, 0))
    ps_ref[...] = pr + (prev - pr) * mu_ref[...]
    carry_ref[...] = pr[tm - 8:tm, :]

    cos = cos_ref[...]
    sin_signed = sin_ref[...]
    lane = lax.broadcasted_iota(I32, (1, LANES), 1)
    lane_lo = (lane % HEAD_DIM) < HEAD_DIM // 2

    pa = _dot(ub, wa_ref[...])
    for g in range(WIDTH // LANES):
        sl = slice(g * LANES, (g + 1) * LANES)
        qg = _rope_group(pa[:, sl], cos, sin_signed, lane_lo) * (HEAD_DIM ** -0.5 * LOG2_E)
        q_ref[:, sl] = qg.astype(BF16)
        kg = _rope_group(pa[:, WIDTH + g * LANES:WIDTH + (g + 1) * LANES], cos, sin_signed, lane_lo)
        k_ref[:, sl] = kg.astype(BF16)
    v_ref[...] = pa[:, 2 * WIDTH:].astype(BF16)

    pi = _dot(ub, wi_ref[...])
    for g in range(WIDTH // LANES):
        sl = slice(g * LANES, (g + 1) * LANES)
        qi_ref[:, sl] = _rope_group(pi[:, sl], cos, sin_signed, lane_lo).astype(BF16)
    tail = pi[:, WIDTH:WIDTH + LANES]
    kw_ref[...] = tail
    kr = _rope_group(tail, cos, sin_signed, lane_lo)
    ki_ref[...] = jnp.where(lane < HEAD_DIM, kr, pltpu.roll(kr, HEAD_DIM, 1)).astype(BF16)

    gate_ref[...] = _sigmoid(_dot(ub, wg_ref[...]))


def _rope_tables(seq):
    half = HEAD_DIM // 2
    inv = 1.0 / (ROPE_THETA ** (jnp.arange(half, dtype=F32) * 2.0 / HEAD_DIM))
    ang = jnp.arange(seq, dtype=F32)[:, None] * inv[None, :]
    cos, sin = jnp.cos(ang), jnp.sin(ang)
    cos_t = jnp.concatenate([cos, cos, cos, cos], axis=1)
    sin_t = jnp.concatenate([-sin, sin, -sin, sin], axis=1)
    return cos_t, sin_t


def _full(shape):
    return pl.BlockSpec(shape, lambda *_: (0,) * len(shape))


def _project(x2, norm1_g, w_in, tshift_mu, seq):
    tokens = x2.shape[0]
    tm = 512 if seq % 512 == 0 else seq
    c0, c1, c2 = RWKV_COLS, RWKV_COLS + ATT_COLS, RWKV_COLS + ATT_COLS + IDX_COLS
    wr = w_in[:, :c0].astype(BF16)
    wa = w_in[:, c0:c1].astype(BF16)
    wi = jnp.pad(w_in[:, c1:c2], ((0, 0), (0, IDX_PAD - IDX_COLS))).astype(BF16)
    wg = w_in[:, c2:].astype(BF16)
    cos_t, sin_t = _rope_tables(seq)
    tps = seq // tm
    row = lambda w: pl.BlockSpec((tm, w), lambda i: (i, 0))
    pos = pl.BlockSpec((tm, LANES), lambda i: (i % tps, 0))
    out_shape = (
        jax.ShapeDtypeStruct((tokens, RWKV_COLS), F32),
        jax.ShapeDtypeStruct((tokens, WIDTH), BF16),
        jax.ShapeDtypeStruct((tokens, WIDTH), BF16),
        jax.ShapeDtypeStruct((tokens, WIDTH), BF16),
        jax.ShapeDtypeStruct((tokens, WIDTH), BF16),
        jax.ShapeDtypeStruct((tokens, LANES), BF16),
        jax.ShapeDtypeStruct((tokens, LANES), F32),
        jax.ShapeDtypeStruct((tokens, GATE_COLS), F32),
    )
    return pl.pallas_call(
        functools.partial(_proj_kernel, tiles_per_seq=tps),
        grid=(tokens // tm,),
        in_specs=[row(D_MODEL), _full((1, D_MODEL)), _full((D_MODEL, RWKV_COLS)),
                  _full((D_MODEL, ATT_COLS)), _full((D_MODEL, IDX_PAD)), _full((D_MODEL, GATE_COLS)),
                  _full((1, RWKV_COLS)), pos, pos],
        out_specs=(row(RWKV_COLS), row(WIDTH), row(WIDTH), row(WIDTH), row(WIDTH),
                   row(LANES), row(LANES), row(GATE_COLS)),
        out_shape=out_shape,
        scratch_shapes=[pltpu.VMEM((8, RWKV_COLS), F32)],
        compiler_params=pltpu.CompilerParams(dimension_semantics=("arbitrary",),
                                             vmem_limit_bytes=VMEM_LIMIT),
        name="proj",
    )(x2, norm1_g.reshape(1, -1), wr, wa, wi, wg, tshift_mu.reshape(1, -1), cos_t, sin_t)


RWKV_CHUNK = 64
RWKV_CHUNKS_PER_STEP = 2
HI = lax.Precision.HIGHEST


def _pair_sum(x, lane_first):
    s0 = jnp.sum(jnp.where(lane_first, x, 0.0), axis=1, keepdims=True)
    s1 = jnp.sum(jnp.where(lane_first, 0.0, x), axis=1, keepdims=True)
    return jnp.where(lane_first, s0, s1)


def _rwkv_kernel(ps_ref, wdu_ref, w0_ref, aup_ref, a0_ref, gup_ref, kk_ref, ka_ref, rk_ref,
                 lng_ref, lnb_ref, y_ref, state_ref):
    c = pl.program_id(1)
    L = RWKV_CHUNK

    @pl.when(c == 0)
    def _():
        state_ref[...] = jnp.zeros_like(state_ref)

    ps = ps_ref[...]
    r_all = ps[:, :WIDTH]
    k_all = ps[:, WIDTH:2 * WIDTH]
    v_all = ps[:, 2 * WIDTH:3 * WIDTH]
    o = 3 * WIDTH
    wd = ps[:, o:o + DECAY_RANK]
    ad = ps[:, o + DECAY_RANK:o + DECAY_RANK + AAA_RANK]
    gd = ps[:, o + DECAY_RANK + AAA_RANK:]

    bf = lambda t: t.astype(BF16)
    z = -(w0_ref[...] + _dot(bf(jnp.tanh(wd)), wdu_ref[...]))
    softplus = jnp.maximum(z, 0.0) + jnp.log(1.0 + jnp.exp(-jnp.abs(z)))
    w_all = -softplus - 0.5
    ld_all = -jnp.exp(w_all)
    a_all = _sigmoid(a0_ref[...] + _dot(bf(ad), aup_ref[...]))
    g_all = _dot(bf(_sigmoid(gd)), gup_ref[...])

    lane = lax.broadcasted_iota(I32, (1, LANES), 1)
    lane_first = lane < HEAD_DIM
    ti = lax.broadcasted_iota(I32, (L, 2 * L), 0)
    tj = lax.broadcasted_iota(I32, (L, 2 * L), 1) % L
    strict = ti > tj
    incl = ti >= tj
    bi = lax.broadcasted_iota(I32, (LANES, 2 * LANES), 0)
    bj = lax.broadcasted_iota(I32, (LANES, 2 * LANES), 1) % LANES
    same_head = (bi < HEAD_DIM) == (bj < HEAD_DIM)
    eye = (bi == bj)[:, :LANES]
    tri_ones = jnp.where(incl[:, :L], 1.0, 0.0)
    zeros_l = jnp.zeros((L, LANES), BF16)

    n_pairs = WIDTH // LANES
    lane2_first = jnp.concatenate([lane_first, lane_first], axis=1)
    pr = {}
    for ci in range(RWKV_CHUNKS_PER_STEP):
        rows = slice(ci * L, (ci + 1) * L)
        for g in range(n_pairs):
            sl = slice(g * LANES, (g + 1) * LANES)
            r, k, v = r_all[rows, sl], k_all[rows, sl], v_all[rows, sl]
            a, ld = a_all[rows, sl], ld_all[rows, sl]
            kk = k * kk_ref[:, sl]
            nrm = jnp.sqrt(_pair_sum(kk * kk, lane_first))
            kk = kk / jnp.maximum(nrm, 1e-12)
            k = k * (1.0 + (a - 1.0) * ka_ref[:, sl])
            cum = _dot(tri_ones, ld, HI)
            p_inc = jnp.exp(cum)
            p_inv = jnp.exp(-cum)
            al = -kk * jnp.exp(cum - ld)
            be = kk * a * p_inv
            kt = k * p_inv
            rt = r * p_inc
            p_last = p_inc[L - 1:L, :]
            v_b = bf(v)
            pr[ci, g] = dict(
                r=r, k=k, v=v, al=al, rt=rt, p_last=p_last,
                bk=bf(jnp.concatenate([be, kt], axis=0)),
                bkp=bf(jnp.concatenate([be * p_last, kt * p_last], axis=0)),
                zv=jnp.concatenate([zeros_l, v_b], axis=0),
                zv2=jnp.concatenate([zeros_l, v_b], axis=1))

    heads = [(ci, g, hh) for ci in range(RWKV_CHUNKS_PER_STEP) for g in range(n_pairs)
             for hh in range(2)]
    gram = []
    for ci, g, hh in heads:
        mask = lane_first if hh == 0 else jnp.logical_not(lane_first)
        ar = bf(jnp.concatenate([jnp.where(mask, pr[ci, g]["al"], 0.0),
                                 jnp.where(mask, pr[ci, g]["rt"], 0.0)], axis=0))
        gram.append(_dot_nt(ar, pr[ci, g]["bk"]))
    top = [jnp.where(strict, gm[:L], 0.0) for gm in gram]
    bot = [bf(jnp.where(incl, gm[L:], 0.0)) for gm in gram]
    rhs = [jnp.concatenate([pr[ci, g]["al"], _dot(bf(top[i]), pr[ci, g]["zv"])], axis=1)
           for i, (ci, g, hh) in enumerate(heads)]
    npow = [bf(t[:, :L]) for t in top]
    u = [rhs[i] + _dot(npow[i], bf(rhs[i])) for i in range(len(heads))]
    for _ in range(int(math.log2(L)) - 1):
        npow = [bf(_dot(n, n)) for n in npow]
        u = [u[i] + _dot(npow[i], bf(u[i])) for i in range(len(heads))]
    u_b = [bf(x) for x in u]
    o_h = [_dot(bot[i], jnp.concatenate([u_b[i], pr[ci, g]["zv2"]], axis=0))
           for i, (ci, g, hh) in enumerate(heads)]

    for g in range(n_pairs):
        sl = slice(g * LANES, (g + 1) * LANES)
        st = state_ref[g]
        for ci in range(RWKV_CHUNKS_PER_STEP):
            rows = slice(ci * L, (ci + 1) * L)
            p = pr[ci, g]
            i0 = (ci * n_pairs + g) * 2
            ub = jnp.where(lane2_first, u_b[i0], u_b[i0 + 1])
            o2 = jnp.where(lane2_first, o_h[i0], o_h[i0 + 1])
            ry = p["rt"] + o2[:, :LANES]
            yl = o2[:, LANES:]
            uz = jnp.concatenate([ub, p["zv2"]], axis=0)
            mg = jnp.where(same_head, _dot_tn(p["bkp"], uz), 0.0)
            m_mat = jnp.where(eye, p["p_last"], 0.0) + mg[:, :LANES]
            g_mat = mg[:, LANES:]

            st_b = bf(st)
            y = _dot(bf(ry), st_b) + yl
            st = _dot(bf(m_mat), st_b) + g_mat

            mean = _pair_sum(y, lane_first) * (1.0 / HEAD_DIM)
            yc = y - mean
            var = _pair_sum(yc * yc, lane_first) * (1.0 / HEAD_DIM)
            yn = yc * lax.rsqrt(var + LNX_EPS) * lng_ref[:, sl] + lnb_ref[:, sl]
            bonus = _pair_sum(p["r"] * p["k"] * rk_ref[:, sl], lane_first) * p["v"]
            y_ref[rows, sl] = (yn + bonus) * g_all[rows, sl]
        state_ref[g] = st


def _rwkv(ps, w_decay_up, w0, a_up, a0, g_up, k_k, k_a, r_k, lnx_g, lnx_b, batch, seq):
    L = RWKV_CHUNK * RWKV_CHUNKS_PER_STEP
    n_chunks = seq // L
    vec = lambda t: t.reshape(1, WIDTH)
    return pl.pallas_call(
        _rwkv_kernel,
        grid=(batch, n_chunks),
        in_specs=[pl.BlockSpec((L, RWKV_COLS), lambda b, c: (b * n_chunks + c, 0)),
                  _full((DECAY_RANK, WIDTH)), _full((1, WIDTH)), _full((AAA_RANK, WIDTH)),
                  _full((1, WIDTH)), _full((GATE_RANK, WIDTH)), _full((1, WIDTH)),
                  _full((1, WIDTH)), _full((1, WIDTH)), _full((1, WIDTH)), _full((1, WIDTH))],
        out_specs=pl.BlockSpec((L, WIDTH), lambda b, c: (b * n_chunks + c, 0)),
        out_shape=jax.ShapeDtypeStruct((batch * seq, WIDTH), F32),
        scratch_shapes=[pltpu.VMEM((WIDTH // LANES, LANES, LANES), F32)],
        compiler_params=pltpu.CompilerParams(dimension_semantics=("arbitrary", "arbitrary"),
                                             vmem_limit_bytes=VMEM_LIMIT),
        name="rwkv",
    )(ps, w_decay_up.astype(BF16), vec(w0), a_up.astype(BF16), vec(a0), g_up.astype(BF16),
      vec(k_k), vec(k_a), vec(r_k), vec(lnx_g), vec(lnx_b))


DSA_TQ = LANES
DSA_KC = 512
DSA_RB = 512
DSA_PASSES_PER_TEST = 4


def _dsa_kernel(qi_ref, kw_ref, q_ref, ki_ref, k_ref, v_ref, o_ref, keys_ref, acc_ref,
                s_ref, bias_ref, *, seq, topk, kc):
    tq = DSA_TQ
    qb = pl.program_id(1)
    n_kc = (qb * tq + tq + kc - 1) // kc
    t_col = qb * tq + lax.broadcasted_iota(I32, (1, tq), 1)
    row0 = lax.broadcasted_iota(I32, (kc, 1), 0)
    lane = lax.broadcasted_iota(I32, (1, LANES), 1)
    lane_first = lane < HEAD_DIM
    idx_scale = (HEAD_DIM ** -0.5) * (N_HEADS ** -0.5)
    zero = jnp.zeros((), BF16)
    n_pairs = WIDTH // LANES

    def pair_stack(x):
        out = []
        for g in range(WIDTH // LANES):
            xg = x[:, g * LANES:(g + 1) * LANES]
            out.append(jnp.concatenate([jnp.where(lane_first, xg, zero),
                                        jnp.where(lane_first, zero, xg)], axis=0))
        return out

    qi2 = pair_stack(qi_ref[...])
    kw_t = kw_ref[...].T
    w_h = [kw_t[HEAD_DIM + h:HEAD_DIM + h + 1, :] * idx_scale for h in range(N_HEADS)]

    def stage_scores(c, g, base):
        off = pl.multiple_of(c * kc, kc)
        s_ref[base + g] = _dot_nt(ki_ref[pl.ds(off, kc), :], qi2[g])

    def fold_scores(g, base, acc):
        s2 = s_ref[base + g]
        acc = acc + jnp.maximum(s2[:, :tq], 0.0) * w_h[2 * g]
        return acc + jnp.maximum(s2[:, tq:], 0.0) * w_h[2 * g + 1]

    def store_keys(c, acc):
        off = pl.multiple_of(c * kc, kc)
        score = jnp.where(acc == 0.0, 0.0, acc)
        keys_ref[pl.ds(off, kc), :] = jnp.where(off + row0 <= t_col, score, -jnp.inf)

    def score_step(c, src, dst, stage_next):
        acc = jnp.zeros((kc, tq), F32)
        for g in range(n_pairs):
            acc = fold_scores(g, src, acc)
            if stage_next:
                stage_scores(c + 1, g, dst)
        store_keys(c, acc)

    for g in range(n_pairs):
        stage_scores(0, g, 0)
    n_double = (n_kc - 1) // 2

    def score_double_step(i, carry):
        score_step(2 * i, 0, n_pairs, True)
        score_step(2 * i + 1, n_pairs, 0, True)
        return carry

    lax.fori_loop(0, n_double, score_double_step, 0)
    c_rest = 2 * n_double

    @pl.when(n_kc - c_rest == 1)
    def _():
        score_step(c_rest, 0, n_pairs, False)

    @pl.when(n_kc - c_rest == 2)
    def _():
        score_step(c_rest, 0, n_pairs, True)
        score_step(c_rest + 1, n_pairs, 0, False)

    rb = DSA_RB
    n_blk = n_kc * (kc // rb)
    rrow = lax.broadcasted_iota(I32, (rb, 1), 0)

    def scan(fn, init):
        def body(j, carry):
            return fn(keys_ref[pl.ds(pl.multiple_of(j * rb, rb), rb), :], j, carry)
        return lax.fori_loop(0, n_blk, body, init)

    def tree(x, op):
        parts = [x[r * 8:(r + 1) * 8, :] for r in range(rb // 8)]
        while len(parts) > 1:
            parts = [op(parts[i], parts[i + 1]) for i in range(0, len(parts), 2)]
        return parts[0]

    def count_hits(hits):
        cnt = scan(lambda blk, j, cnt: cnt + tree(hits(blk, j), jnp.add), jnp.zeros((8, tq), F32))
        return jnp.sum(cnt, axis=0, keepdims=True)

    def count_ge(cand):
        return count_hits(lambda blk, j: jnp.where(blk >= cand, 1.0, 0.0))

    kf = float(topk)
    tiny = float(np.finfo(np.float32).tiny)
    n_valid = (t_col + 1).astype(F32)

    def first_pass(blk, j, carry):
        cnt, mx = carry
        return (cnt + tree(jnp.where(blk >= 0.0, 1.0, 0.0), jnp.add), jnp.maximum(mx, tree(blk, jnp.maximum)))

    cnt0, mx = scan(first_pass, (jnp.zeros((8, tq), F32), jnp.full((8, tq), -jnp.inf, F32)))
    n_pos = jnp.sum(cnt0, axis=0, keepdims=True)
    top = jnp.max(mx, axis=0, keepdims=True)
    tbits = pltpu.bitcast(top, I32)
    above = pltpu.bitcast(jnp.where(top > 0.0, tbits + 1, tbits - 1), F32)
    above = jnp.where(top == 0.0, tiny, above)
    nonneg = n_pos >= kf

    def lowest():
        def body(blk, j, mn):
            return jnp.minimum(mn, tree(jnp.where(blk == -jnp.inf, jnp.inf, blk), jnp.minimum))
        mn = scan(body, jnp.full((8, tq), jnp.inf, F32))
        return jnp.min(mn, axis=0, keepdims=True)

    need_low = jnp.max(jnp.where(jnp.logical_and(jnp.logical_not(nonneg), n_valid > kf), 1.0, 0.0)) > 0.0
    low = lax.cond(need_low, lowest, lambda: jnp.zeros((1, tq), F32))

    short = n_valid <= kf
    lo0 = jnp.where(short, -jnp.inf, jnp.where(nonneg, 0.0, low))
    hi0 = jnp.where(nonneg, above, 0.0)
    c_lo0 = jnp.where(short, n_valid, jnp.where(nonneg, n_pos, n_valid))
    done0 = jnp.where(jnp.logical_or(short, c_lo0 == kf), 1.0, 0.0)

    def bisect_cond(st):
        i, lo, hi, c_lo, done = st
        return jnp.min(done) < 0.5

    def bisect_body(st):
        i, lo, hi, c_lo, done = st
        for _ in range(DSA_PASSES_PER_TEST):
            mid = lo * 0.5 + hi * 0.5
            mid = jnp.where(i == 0, jnp.where(lo == 0.0, tiny, jnp.where(hi == 0.0, -tiny, mid)), mid)
            stuck = jnp.logical_or(mid <= lo, mid >= hi)
            n = count_ge(mid)
            live = jnp.logical_and(done < 0.5, jnp.logical_not(stuck))
            up = jnp.logical_and(live, n >= kf)
            lo = jnp.where(up, mid, lo)
            c_lo = jnp.where(up, n, c_lo)
            hi = jnp.where(jnp.logical_and(live, n < kf), mid, hi)
            done = jnp.where(jnp.logical_or(stuck, jnp.logical_and(live, n == kf)), 1.0, done)
            i = i + 1
        return i, lo, hi, c_lo, done

    _, thr, _, n_ge, _ = lax.while_loop(bisect_cond, bisect_body,
                                        (jnp.int32(0), lo0, hi0, c_lo0, done0))

    idx_bits = max(1, int(math.ceil(math.log2(seq))))

    def tie_limit():
        need = kf - count_hits(lambda blk, j: jnp.where(blk > thr, 1.0, 0.0))

        def step(i, lim):
            cand = lim | (jnp.int32(1) << (idx_bits - 1 - i))
            below = count_hits(lambda blk, j: jnp.where(
                blk == thr, jnp.where(j * rb + rrow < cand, 1.0, 0.0), 0.0))
            return jnp.where(below < need, cand, lim)
        return lax.fori_loop(0, idx_bits, step, jnp.zeros((1, tq), I32))

    has_excess = jnp.max(n_ge - kf) > 0.0
    lim = lax.cond(has_excess, tie_limit, lambda: jnp.full((1, tq), seq, I32))

    q2 = pair_stack(q_ref[...])

    def attention(plain):
        acc_ref[...] = jnp.zeros_like(acc_ref)

        def make_bias(c):
            off = pl.multiple_of(c * kc, kc)
            key = keys_ref[pl.ds(off, kc), :]
            if plain:
                bias_ref[...] = jnp.where(key >= thr, 0.0, NEG_BIG)
            else:
                rowi = off + row0
                tie_ok = jnp.where(rowi <= lim, 0.0, NEG_BIG)
                bias = jnp.where(key > thr, 0.0, jnp.where(key == thr, tie_ok, NEG_BIG))
                bias_ref[...] = jnp.where(rowi <= t_col, bias, NEG_BIG)

        def logits(c, g, base):
            off = pl.multiple_of(c * kc, kc)
            kg = k_ref[pl.ds(off, kc), g * LANES:(g + 1) * LANES]
            bias = bias_ref[...]
            s2 = _dot_nt(kg, q2[g]) + jnp.concatenate([bias, bias], axis=1)
            s_ref[base + g] = s2
            return jnp.max(s2, axis=0, keepdims=True)

        def consume(c, g, base, m_old, l_old, m_chunk):
            off = pl.multiple_of(c * kc, kc)
            vg = v_ref[pl.ds(off, kc), g * LANES:(g + 1) * LANES]
            m_new = jnp.maximum(m_old, m_chunk)
            alpha = jnp.exp2(m_old - m_new)
            p = jnp.exp2(s_ref[base + g] - m_new)
            l_new = alpha * l_old + jnp.sum(p, axis=0, keepdims=True)
            acc_ref[g] = alpha * acc_ref[g] + _dot_tn(vg, p.astype(BF16))
            return m_new, l_new

        def step(c, src, dst, state, stage_next):
            ms, ls, mcs = state
            if stage_next:
                make_bias(c + 1)
            new_m, new_l, new_mc = [], [], []
            for g in range(n_pairs):
                m_new, l_new = consume(c, g, src, ms[g], ls[g], mcs[g])
                new_m.append(m_new)
                new_l.append(l_new)
                if stage_next:
                    new_mc.append(logits(c + 1, g, dst))
            return tuple(new_m), tuple(new_l), tuple(new_mc) if stage_next else mcs

        def finish(state):
            ls = state[1]
            for g in range(n_pairs):
                o2 = acc_ref[g] / ls[g]
                o_ref[:, g * LANES:(g + 1) * LANES] = jnp.where(lane_first, o2[:, :tq].T,
                                                                o2[:, tq:].T)

        make_bias(0)
        state = (tuple(jnp.full((1, 2 * tq), NEG_BIG, F32) for _ in range(n_pairs)),
                 tuple(jnp.zeros((1, 2 * tq), F32) for _ in range(n_pairs)),
                 tuple(logits(0, g, 0) for g in range(n_pairs)))
        n_double = (n_kc - 1) // 2

        def double_step(i, state):
            state = step(2 * i, 0, n_pairs, state, True)
            return step(2 * i + 1, n_pairs, 0, state, True)

        state = lax.fori_loop(0, n_double, double_step, state)
        c_rest = 2 * n_double

        @pl.when(n_kc - c_rest == 1)
        def _():
            finish(step(c_rest, 0, n_pairs, state, False))

        @pl.when(n_kc - c_rest == 2)
        def _():
            st = step(c_rest, 0, n_pairs, state, True)
            finish(step(c_rest + 1, n_pairs, 0, st, False))

    plain = jnp.logical_and(jnp.logical_not(has_excess), qb * tq + 1 >= topk)
    pl.when(plain)(lambda: attention(True))
    pl.when(jnp.logical_not(plain))(lambda: attention(False))


def _dsa(qi, kw, q, ki, k, v, batch, seq):
    tq = DSA_TQ
    kc = DSA_KC if seq % DSA_KC == 0 else DSA_RB
    nqb = seq // tq
    topk = min(MAX_TOPK, seq // 4)
    qrow = lambda w: pl.BlockSpec((tq, w), lambda b, i: (b * nqb + i, 0))
    kvrow = lambda w: pl.BlockSpec((seq, w), lambda b, i: (b, 0))
    return pl.pallas_call(
        functools.partial(_dsa_kernel, seq=seq, topk=topk, kc=kc),
        grid=(batch, nqb),
        in_specs=[qrow(WIDTH), qrow(LANES), qrow(WIDTH), kvrow(LANES), kvrow(WIDTH), kvrow(WIDTH)],
        out_specs=qrow(WIDTH),
        out_shape=jax.ShapeDtypeStruct((batch * seq, WIDTH), F32),
        scratch_shapes=[pltpu.VMEM((seq, tq), F32),
                        pltpu.VMEM((WIDTH // LANES, LANES, 2 * tq), F32),
                        pltpu.VMEM((2 * (WIDTH // LANES), kc, 2 * tq), F32),
                        pltpu.VMEM((kc, tq), F32)],
        compiler_params=pltpu.CompilerParams(dimension_semantics=("arbitrary", "arbitrary"),
                                             vmem_limit_bytes=VMEM_LIMIT),
        name="dsa",
    )(qi, kw, q, ki, k, v)


def _tail_kernel(x_ref, ya_ref, yb_ref, gate_ref, wor_ref, woa_ref, wout_ref, g2_ref,
                 wfg_ref, wfu_ref, wfo_ref, gf_ref, o_ref):
    gate = gate_ref[...]
    ma = _dot(ya_ref[...].astype(BF16), wor_ref[...])
    mb = _dot(yb_ref[...].astype(BF16), woa_ref[...])
    merged = gate[:, :D_MODEL] * ma + gate[:, D_MODEL:] * mb
    h = x_ref[...] + _dot(merged.astype(BF16), wout_ref[...])
    hn = h * lax.rsqrt(jnp.mean(h * h, axis=-1, keepdims=True) + NORM_EPS) * g2_ref[...]
    hb = hn.astype(BF16)
    ffn = jnp.zeros_like(h)
    half = D_FF // 2
    for f in range(2):
        sl = slice(f * half, (f + 1) * half)
        zg = _dot(hb, wfg_ref[:, sl])
        zu = _dot(hb, wfu_ref[:, sl])
        act = zg * _sigmoid(zg) * zu
        ffn = ffn + _dot(act.astype(BF16), wfo_ref[sl, :])
    h = h + ffn
    o_ref[...] = h * lax.rsqrt(jnp.mean(h * h, axis=-1, keepdims=True) + NORM_EPS) * gf_ref[...]


def _tail(x2, ya, yb, gate, w_o_rwkv, w_o_att, w_out, norm2_g, w_ffn_in, w_ffn_out, normf_g):
    tokens = x2.shape[0]
    tm = 512 if tokens % 512 == 0 else tokens
    row = lambda w: pl.BlockSpec((tm, w), lambda i: (i, 0))
    const = lambda shape: pl.BlockSpec(shape, lambda i: (0, 0), pipeline_mode=pl.Buffered(1))
    return pl.pallas_call(
        _tail_kernel,
        grid=(tokens // tm,),
        in_specs=[row(D_MODEL), row(WIDTH), row(WIDTH), row(GATE_COLS),
                  const((WIDTH, D_MODEL)), const((WIDTH, D_MODEL)), const((D_MODEL, D_MODEL)),
                  const((1, D_MODEL)), const((D_MODEL, D_FF)), const((D_MODEL, D_FF)),
                  const((D_FF, D_MODEL)), const((1, D_MODEL))],
        out_specs=row(D_MODEL),
        out_shape=jax.ShapeDtypeStruct((tokens, D_MODEL), F32),
        compiler_params=pltpu.CompilerParams(dimension_semantics=("arbitrary",),
                                             vmem_limit_bytes=VMEM_LIMIT),
        name="tail",
    )(x2, ya, yb, gate, w_o_rwkv.astype(BF16), w_o_att.astype(BF16), w_out.astype(BF16),
      norm2_g.reshape(1, -1), w_ffn_in[:, :D_FF].astype(BF16), w_ffn_in[:, D_FF:].astype(BF16),
      w_ffn_out.astype(BF16), normf_g.reshape(1, -1))


def kernel(x, norm1_g, w_in, tshift_mu, w_decay_up, w0, a_up, a0, g_up, k_k, k_a, r_k, lnx_g, lnx_b, w_o_rwkv, w_o_att, w_out, norm2_g, w_ffn_in, w_ffn_out, normf_g):
    batch, seq, _ = x.shape
    x2 = x.reshape(batch * seq, D_MODEL)
    ps, q, k, v, qi, ki, kw, gate = _project(x2, norm1_g, w_in, tshift_mu, seq)
    ya = _rwkv(ps, w_decay_up, w0, a_up, a0, g_up, k_k, k_a, r_k.reshape(-1), lnx_g, lnx_b,
               batch, seq)
    yb = _dsa(qi, kw, q, ki, k, v, batch, seq)
    out = _tail(x2, ya, yb, gate, w_o_rwkv, w_o_att, w_out, norm2_g, w_ffn_in, w_ffn_out, normf_g)
    return out.reshape(batch, seq, D_MODEL)
```

```python
import functools
import math

import jax
import jax.numpy as jnp
import numpy as np
from jax import lax
from jax.experimental import pallas as pl
from jax.experimental.pallas import tpu as pltpu

F32 = jnp.float32
BF16 = jnp.bfloat16
I32 = jnp.int32

D_MODEL = 1024
HEAD_DIM = 64
N_HEADS = 8
WIDTH = N_HEADS * HEAD_DIM
DECAY_RANK = 64
AAA_RANK = 64
GATE_RANK = 128
MAX_TOPK = 256
ROPE_THETA = 10000.0
D_FF = 2816
NORM_EPS = 1e-6
LNX_EPS = 64e-5
RWKV_COLS = 3 * WIDTH + DECAY_RANK + AAA_RANK + GATE_RANK
ATT_COLS = 3 * WIDTH
IDX_COLS = N_HEADS * HEAD_DIM + HEAD_DIM + N_HEADS
IDX_PAD = 640
GATE_COLS = 2 * D_MODEL
LANES = 128
VMEM_LIMIT = 56 * 1024 * 1024
TAIL_SPLIT = 2

INT_MIN = -(2 ** 31)
LOG2_E = 1.4426950408889634
NEG_BIG = -1e30


def _dot(a, b, precision=None):
    return jnp.dot(a, b, preferred_element_type=F32, precision=precision)


def _dot_nt(a, b, precision=None):
    return lax.dot_general(a, b, (((1,), (1,)), ((), ())), preferred_element_type=F32,
                           precision=precision)


def _dot_tn(a, b, precision=None):
    return lax.dot_general(a, b, (((0,), (0,)), ((), ())), preferred_element_type=F32,
                           precision=precision)


def _sigmoid(x):
    return 1.0 / (1.0 + jnp.exp(-x))


def _rope_group(xg, cos, sin_signed, lane_lo):
    partner = jnp.where(lane_lo, pltpu.roll(xg, LANES - HEAD_DIM // 2, 1),
                        pltpu.roll(xg, HEAD_DIM // 2, 1))
    return xg * cos + partner * sin_signed


def _proj_kernel(x_ref, g1_ref, wr_ref, wa_ref, wi_ref, wg_ref, mu_ref, cos_ref, sin_ref,
                 ps_ref, q_ref, k_ref, v_ref, qi_ref, ki_ref, kw_ref, gate_ref, carry_ref,
                 *, tiles_per_seq):
    i = pl.program_id(0)

    @pl.when(i % tiles_per_seq == 0)
    def _():
        carry_ref[...] = jnp.zeros_like(carry_ref)

    x = x_ref[...]
    tm = x.shape[0]
    u = x * lax.rsqrt(jnp.mean(x * x, axis=-1, keepdims=True) + NORM_EPS) * g1_ref[...]
    ub = u.astype(BF16)
    cos = cos_ref[...]
    sin_signed = sin_ref[...]
    lane = lax.broadcasted_iota(I32, (1, LANES), 1)
    lane_lo = (lane % HEAD_DIM) < HEAD_DIM // 2
    row = lax.broadcasted_iota(I32, (tm, 1), 0)

    def rwkv_epilogue(pr):
        prev = jnp.where(row == 0, carry_ref[7:8, :], pltpu.roll(pr, 1, 0))
        ps_ref[...] = pr + (prev - pr) * mu_ref[...]
        carry_ref[...] = pr[tm - 8:tm, :]

    def att_epilogue(pa):
        for g in range(WIDTH // LANES):
            sl = slice(g * LANES, (g + 1) * LANES)
            qg = _rope_group(pa[:, sl], cos, sin_signed, lane_lo) * (HEAD_DIM ** -0.5 * LOG2_E)
            q_ref[:, sl] = qg.astype(BF16)
            kg = _rope_group(pa[:, WIDTH + g * LANES:WIDTH + (g + 1) * LANES], cos, sin_signed,
                             lane_lo)
            k_ref[:, sl] = kg.astype(BF16)
        v_ref[...] = pa[:, 2 * WIDTH:].astype(BF16)

    def idx_epilogue(pi):
        for g in range(WIDTH // LANES):
            sl = slice(g * LANES, (g + 1) * LANES)
            qi_ref[:, sl] = _rope_group(pi[:, sl], cos, sin_signed, lane_lo).astype(BF16)
        tail = pi[:, WIDTH:WIDTH + LANES]
        kw_ref[...] = tail
        kr = _rope_group(tail, cos, sin_signed, lane_lo)
        ki_ref[...] = jnp.where(lane < HEAD_DIM, kr, pltpu.roll(kr, HEAD_DIM, 1)).astype(BF16)

    pr = _dot(ub, wr_ref[...])
    pa = _dot(ub, wa_ref[...])
    rwkv_epilogue(pr)
    pi = _dot(ub, wi_ref[...])
    att_epilogue(pa)
    pg = _dot(ub, wg_ref[...])
    idx_epilogue(pi)
    gate_ref[...] = _sigmoid(pg)


def _rope_tables(seq):
    half = HEAD_DIM // 2
    inv = 1.0 / (ROPE_THETA ** (jnp.arange(half, dtype=F32) * 2.0 / HEAD_DIM))
    ang = jnp.arange(seq, dtype=F32)[:, None] * inv[None, :]
    cos, sin = jnp.cos(ang), jnp.sin(ang)
    cos_t = jnp.concatenate([cos, cos, cos, cos], axis=1)
    sin_t = jnp.concatenate([-sin, sin, -sin, sin], axis=1)
    return cos_t, sin_t


def _full(shape):
    return pl.BlockSpec(shape, lambda *_: (0,) * len(shape))


def _project(x2, norm1_g, w_in, tshift_mu, seq):
    tokens = x2.shape[0]
    tm = 512 if seq % 512 == 0 else seq
    c0, c1, c2 = RWKV_COLS, RWKV_COLS + ATT_COLS, RWKV_COLS + ATT_COLS + IDX_COLS
    wr = w_in[:, :c0].astype(BF16)
    wa = w_in[:, c0:c1].astype(BF16)
    wi = jnp.pad(w_in[:, c1:c2], ((0, 0), (0, IDX_PAD - IDX_COLS))).astype(BF16)
    wg = w_in[:, c2:].astype(BF16)
    cos_t, sin_t = _rope_tables(seq)
    tps = seq // tm
    row = lambda w: pl.BlockSpec((tm, w), lambda i: (i, 0))
    pos = pl.BlockSpec((tm, LANES), lambda i: (i % tps, 0))
    out_shape = (
        jax.ShapeDtypeStruct((tokens, RWKV_COLS), F32),
        jax.ShapeDtypeStruct((tokens, WIDTH), BF16),
        jax.ShapeDtypeStruct((tokens, WIDTH), BF16),
        jax.ShapeDtypeStruct((tokens, WIDTH), BF16),
        jax.ShapeDtypeStruct((tokens, WIDTH), BF16),
        jax.ShapeDtypeStruct((tokens, LANES), BF16),
        jax.ShapeDtypeStruct((tokens, LANES), F32),
        jax.ShapeDtypeStruct((tokens, GATE_COLS), F32),
    )
    return pl.pallas_call(
        functools.partial(_proj_kernel, tiles_per_seq=tps),
        grid=(tokens // tm,),
        in_specs=[row(D_MODEL), _full((1, D_MODEL)), _full((D_MODEL, RWKV_COLS)),
                  _full((D_MODEL, ATT_COLS)), _full((D_MODEL, IDX_PAD)), _full((D_MODEL, GATE_COLS)),
                  _full((1, RWKV_COLS)), pos, pos],
        out_specs=(row(RWKV_COLS), row(WIDTH), row(WIDTH), row(WIDTH), row(WIDTH),
                   row(LANES), row(LANES), row(GATE_COLS)),
        out_shape=out_shape,
        scratch_shapes=[pltpu.VMEM((8, RWKV_COLS), F32)],
        compiler_params=pltpu.CompilerParams(dimension_semantics=("arbitrary",),
                                             vmem_limit_bytes=VMEM_LIMIT),
        name="proj",
    )(x2, norm1_g.reshape(1, -1), wr, wa, wi, wg, tshift_mu.reshape(1, -1), cos_t, sin_t)


RWKV_CHUNK = 64
RWKV_CHUNKS_PER_STEP = 4
HI = lax.Precision.HIGHEST


def _pair_sum(x, lane_first):
    s0 = jnp.sum(jnp.where(lane_first, x, 0.0), axis=1, keepdims=True)
    s1 = jnp.sum(jnp.where(lane_first, 0.0, x), axis=1, keepdims=True)
    return jnp.where(lane_first, s0, s1)


def _rwkv_kernel(ps_ref, wdu_ref, w0_ref, aup_ref, a0_ref, gup_ref, kk_ref, ka_ref, rk_ref,
                 lng_ref, lnb_ref, y_ref, state_ref):
    c = pl.program_id(1)
    L = RWKV_CHUNK

    @pl.when(c == 0)
    def _():
        state_ref[...] = jnp.zeros_like(state_ref)

    ps = ps_ref[...]
    r_all = ps[:, :WIDTH]
    k_all = ps[:, WIDTH:2 * WIDTH]
    v_all = ps[:, 2 * WIDTH:3 * WIDTH]
    o = 3 * WIDTH
    wd = ps[:, o:o + DECAY_RANK]
    ad = ps[:, o + DECAY_RANK:o + DECAY_RANK + AAA_RANK]
    gd = ps[:, o + DECAY_RANK + AAA_RANK:]

    bf = lambda t: t.astype(BF16)
    z = -(w0_ref[...] + _dot(bf(jnp.tanh(wd)), wdu_ref[...]))
    softplus = jnp.maximum(z, 0.0) + jnp.log(1.0 + jnp.exp(-jnp.abs(z)))
    w_all = -softplus - 0.5
    ld_all = -jnp.exp(w_all)
    a_all = _sigmoid(a0_ref[...] + _dot(bf(ad), aup_ref[...]))
    g_all = _dot(bf(_sigmoid(gd)), gup_ref[...])

    lane = lax.broadcasted_iota(I32, (1, LANES), 1)
    lane_first = lane < HEAD_DIM
    ti = lax.broadcasted_iota(I32, (L, 2 * L), 0)
    tj = lax.broadcasted_iota(I32, (L, 2 * L), 1) % L
    strict = ti > tj
    incl = ti >= tj
    bi = lax.broadcasted_iota(I32, (LANES, 2 * LANES), 0)
    bj = lax.broadcasted_iota(I32, (LANES, 2 * LANES), 1) % LANES
    same_head = (bi < HEAD_DIM) == (bj < HEAD_DIM)
    eye = (bi == bj)[:, :LANES]
    tri_ones = jnp.where(incl[:, :L], 1.0, 0.0)
    zeros_l = jnp.zeros((L, LANES), BF16)

    n_pairs = WIDTH // LANES
    lane2_first = jnp.concatenate([lane_first, lane_first], axis=1)
    pr = {}
    for ci in range(RWKV_CHUNKS_PER_STEP):
        rows = slice(ci * L, (ci + 1) * L)
        for g in range(n_pairs):
            sl = slice(g * LANES, (g + 1) * LANES)
            r, k, v = r_all[rows, sl], k_all[rows, sl], v_all[rows, sl]
            a, ld = a_all[rows, sl], ld_all[rows, sl]
            kk = k * kk_ref[:, sl]
            nrm = jnp.sqrt(_pair_sum(kk * kk, lane_first))
            kk = kk / jnp.maximum(nrm, 1e-12)
            k = k * (1.0 + (a - 1.0) * ka_ref[:, sl])
            cum = _dot(tri_ones, ld, HI)
            p_inc = jnp.exp(cum)
            p_inv = jnp.exp(-cum)
            al = -kk * jnp.exp(cum - ld)
            be = kk * a * p_inv
            kt = k * p_inv
            rt = r * p_inc
            p_last = p_inc[L - 1:L, :]
            v_b = bf(v)
            pr[ci, g] = dict(
                r=r, k=k, v=v, al=al, rt=rt, p_last=p_last,
                bk=bf(jnp.concatenate([be, kt], axis=0)),
                bkp=bf(jnp.concatenate([be * p_last, kt * p_last], axis=0)),
                zv=jnp.concatenate([zeros_l, v_b], axis=0),
                zv2=jnp.concatenate([zeros_l, v_b], axis=1))

    heads = [(ci, g, hh) for ci in range(RWKV_CHUNKS_PER_STEP) for g in range(n_pairs)
             for hh in range(2)]
    gram = []
    for ci, g, hh in heads:
        mask = lane_first if hh == 0 else jnp.logical_not(lane_first)
        ar = bf(jnp.concatenate([jnp.where(mask, pr[ci, g]["al"], 0.0),
                                 jnp.where(mask, pr[ci, g]["rt"], 0.0)], axis=0))
        gram.append(_dot_nt(ar, pr[ci, g]["bk"]))
    top = [jnp.where(strict, gm[:L], 0.0) for gm in gram]
    bot = [bf(jnp.where(incl, gm[L:], 0.0)) for gm in gram]
    rhs = [jnp.concatenate([pr[ci, g]["al"], _dot(bf(top[i]), pr[ci, g]["zv"])], axis=1)
           for i, (ci, g, hh) in enumerate(heads)]
    npow = [bf(t[:, :L]) for t in top]
    u = [rhs[i] + _dot(npow[i], bf(rhs[i])) for i in range(len(heads))]
    for _ in range(int(math.log2(L)) - 1):
        npow = [bf(_dot(n, n)) for n in npow]
        u = [u[i] + _dot(npow[i], bf(u[i])) for i in range(len(heads))]
    u_b = [bf(x) for x in u]
    o_h = [_dot(bot[i], jnp.concatenate([u_b[i], pr[ci, g]["zv2"]], axis=0))
           for i, (ci, g, hh) in enumerate(heads)]

    for g in range(n_pairs):
        sl = slice(g * LANES, (g + 1) * LANES)
        st = state_ref[g]
        for ci in range(RWKV_CHUNKS_PER_STEP):
            rows = slice(ci * L, (ci + 1) * L)
            p = pr[ci, g]
            i0 = (ci * n_pairs + g) * 2
            ub = jnp.where(lane2_first, u_b[i0], u_b[i0 + 1])
            o2 = jnp.where(lane2_first, o_h[i0], o_h[i0 + 1])
            ry = p["rt"] + o2[:, :LANES]
            yl = o2[:, LANES:]
            uz = jnp.concatenate([ub, p["zv2"]], axis=0)
            mg = jnp.where(same_head, _dot_tn(p["bkp"], uz), 0.0)
            m_mat = jnp.where(eye, p["p_last"], 0.0) + mg[:, :LANES]
            g_mat = mg[:, LANES:]

            st_b = bf(st)
            y = _dot(bf(ry), st_b) + yl
            st = _dot(bf(m_mat), st_b) + g_mat

            mean = _pair_sum(y, lane_first) * (1.0 / HEAD_DIM)
            yc = y - mean
            var = _pair_sum(yc * yc, lane_first) * (1.0 / HEAD_DIM)
            yn = yc * lax.rsqrt(var + LNX_EPS) * lng_ref[:, sl] + lnb_ref[:, sl]
            bonus = _pair_sum(p["r"] * p["k"] * rk_ref[:, sl], lane_first) * p["v"]
            y_ref[rows, sl] = (yn + bonus) * g_all[rows, sl]
        state_ref[g] = st


def _rwkv(ps, w_decay_up, w0, a_up, a0, g_up, k_k, k_a, r_k, lnx_g, lnx_b, batch, seq):
    L = RWKV_CHUNK * RWKV_CHUNKS_PER_STEP
    n_chunks = seq // L
    vec = lambda t: t.reshape(1, WIDTH)
    return pl.pallas_call(
        _rwkv_kernel,
        grid=(batch, n_chunks),
        in_specs=[pl.BlockSpec((L, RWKV_COLS), lambda b, c: (b * n_chunks + c, 0)),
                  _full((DECAY_RANK, WIDTH)), _full((1, WIDTH)), _full((AAA_RANK, WIDTH)),
                  _full((1, WIDTH)), _full((GATE_RANK, WIDTH)), _full((1, WIDTH)),
                  _full((1, WIDTH)), _full((1, WIDTH)), _full((1, WIDTH)), _full((1, WIDTH))],
        out_specs=pl.BlockSpec((L, WIDTH), lambda b, c: (b * n_chunks + c, 0)),
        out_shape=jax.ShapeDtypeStruct((batch * seq, WIDTH), F32),
        scratch_shapes=[pltpu.VMEM((WIDTH // LANES, LANES, LANES), F32)],
        compiler_params=pltpu.CompilerParams(dimension_semantics=("arbitrary", "arbitrary"),
                                             vmem_limit_bytes=VMEM_LIMIT),
        name="rwkv",
    )(ps, w_decay_up.astype(BF16), vec(w0), a_up.astype(BF16), vec(a0), g_up.astype(BF16),
      vec(k_k), vec(k_a), vec(r_k), vec(lnx_g), vec(lnx_b))


DSA_TQ = LANES
DSA_KC = 512
DSA_RB = 512
DSA_PASSES_PER_TEST = 4


def _dsa_kernel(qi_ref, kw_ref, q_ref, ki_ref, k_ref, v_ref, o_ref, keys_ref, acc_ref,
                s_ref, bias_ref, *, seq, topk, kc):
    tq = DSA_TQ
    qb = pl.program_id(1)
    n_kc = (qb * tq + tq + kc - 1) // kc
    t_col = qb * tq + lax.broadcasted_iota(I32, (1, tq), 1)
    row0 = lax.broadcasted_iota(I32, (kc, 1), 0)
    lane = lax.broadcasted_iota(I32, (1, LANES), 1)
    lane_first = lane < HEAD_DIM
    idx_scale = (HEAD_DIM ** -0.5) * (N_HEADS ** -0.5)
    zero = jnp.zeros((), BF16)
    n_pairs = WIDTH // LANES

    def pair_stack(x):
        out = []
        for g in range(WIDTH // LANES):
            xg = x[:, g * LANES:(g + 1) * LANES]
            out.append(jnp.concatenate([jnp.where(lane_first, xg, zero),
                                        jnp.where(lane_first, zero, xg)], axis=0))
        return out

    qi2 = pair_stack(qi_ref[...])
    kw_t = kw_ref[...].T
    w_h = [kw_t[HEAD_DIM + h:HEAD_DIM + h + 1, :] * idx_scale for h in range(N_HEADS)]

    def stage_scores(c, g, base):
        off = pl.multiple_of(c * kc, kc)
        s_ref[base + g] = _dot_nt(ki_ref[pl.ds(off, kc), :], qi2[g])

    def fold_scores(g, base, acc):
        s2 = s_ref[base + g]
        acc = acc + jnp.maximum(s2[:, :tq], 0.0) * w_h[2 * g]
        return acc + jnp.maximum(s2[:, tq:], 0.0) * w_h[2 * g + 1]

    def store_keys(c, acc):
        off = pl.multiple_of(c * kc, kc)
        score = jnp.where(acc == 0.0, 0.0, acc)
        keys_ref[pl.ds(off, kc), :] = jnp.where(off + row0 <= t_col, score, -jnp.inf)

    def score_step(c, src, dst, stage_next):
        acc = jnp.zeros((kc, tq), F32)
        for g in range(n_pairs):
            acc = fold_scores(g, src, acc)
            if stage_next:
                stage_scores(c + 1, g, dst)
        store_keys(c, acc)

    for g in range(n_pairs):
        stage_scores(0, g, 0)
    n_double = (n_kc - 1) // 2

    def score_double_step(i, carry):
        score_step(2 * i, 0, n_pairs, True)
        score_step(2 * i + 1, n_pairs, 0, True)
        return carry

    lax.fori_loop(0, n_double, score_double_step, 0)
    c_rest = 2 * n_double

    @pl.when(n_kc - c_rest == 1)
    def _():
        score_step(c_rest, 0, n_pairs, False)

    @pl.when(n_kc - c_rest == 2)
    def _():
        score_step(c_rest, 0, n_pairs, True)
        score_step(c_rest + 1, n_pairs, 0, False)

    rb = DSA_RB
    n_blk = n_kc * (kc // rb)
    rrow = lax.broadcasted_iota(I32, (rb, 1), 0)

    def scan(fn, init):
        def body(j, carry):
            return fn(keys_ref[pl.ds(pl.multiple_of(j * rb, rb), rb), :], j, carry)
        return lax.fori_loop(0, n_blk, body, init)

    def tree(x, op):
        parts = [x[r * 8:(r + 1) * 8, :] for r in range(rb // 8)]
        while len(parts) > 1:
            parts = [op(parts[i], parts[i + 1]) for i in range(0, len(parts), 2)]
        return parts[0]

    def count_hits(hits):
        cnt = scan(lambda blk, j, cnt: cnt + tree(hits(blk, j), jnp.add), jnp.zeros((8, tq), F32))
        return jnp.sum(cnt, axis=0, keepdims=True)

    def count_ge(cand):
        return count_hits(lambda blk, j: jnp.where(blk >= cand, 1.0, 0.0))

    kf = float(topk)
    tiny = float(np.finfo(np.float32).tiny)
    n_valid = (t_col + 1).astype(F32)

    def first_pass(blk, j, carry):
        cnt, mx = carry
        return (cnt + tree(jnp.where(blk >= 0.0, 1.0, 0.0), jnp.add), jnp.maximum(mx, tree(blk, jnp.maximum)))

    cnt0, mx = scan(first_pass, (jnp.zeros((8, tq), F32), jnp.full((8, tq), -jnp.inf, F32)))
    n_pos = jnp.sum(cnt0, axis=0, keepdims=True)
    top = jnp.max(mx, axis=0, keepdims=True)
    tbits = pltpu.bitcast(top, I32)
    above = pltpu.bitcast(jnp.where(top > 0.0, tbits + 1, tbits - 1), F32)
    above = jnp.where(top == 0.0, tiny, above)
    nonneg = n_pos >= kf

    def lowest():
        def body(blk, j, mn):
            return jnp.minimum(mn, tree(jnp.where(blk == -jnp.inf, jnp.inf, blk), jnp.minimum))
        mn = scan(body, jnp.full((8, tq), jnp.inf, F32))
        return jnp.min(mn, axis=0, keepdims=True)

    need_low = jnp.max(jnp.where(jnp.logical_and(jnp.logical_not(nonneg), n_valid > kf), 1.0, 0.0)) > 0.0
    low = lax.cond(need_low, lowest, lambda: jnp.zeros((1, tq), F32))

    short = n_valid <= kf
    lo0 = jnp.where(short, -jnp.inf, jnp.where(nonneg, 0.0, low))
    hi0 = jnp.where(nonneg, above, 0.0)
    c_lo0 = jnp.where(short, n_valid, jnp.where(nonneg, n_pos, n_valid))
    done0 = jnp.where(jnp.logical_or(short, c_lo0 == kf), 1.0, 0.0)

    def bisect_cond(st):
        i, lo, hi, c_lo, done = st
        return jnp.min(done) < 0.5

    def bisect_body(st):
        i, lo, hi, c_lo, done = st
        for _ in range(DSA_PASSES_PER_TEST):
            mid = lo * 0.5 + hi * 0.5
            mid = jnp.where(i == 0, jnp.where(lo == 0.0, tiny, jnp.where(hi == 0.0, -tiny, mid)), mid)
            stuck = jnp.logical_or(mid <= lo, mid >= hi)
            n = count_ge(mid)
            live = jnp.logical_and(done < 0.5, jnp.logical_not(stuck))
            up = jnp.logical_and(live, n >= kf)
            lo = jnp.where(up, mid, lo)
            c_lo = jnp.where(up, n, c_lo)
            hi = jnp.where(jnp.logical_and(live, n < kf), mid, hi)
            done = jnp.where(jnp.logical_or(stuck, jnp.logical_and(live, n == kf)), 1.0, done)
            i = i + 1
        return i, lo, hi, c_lo, done

    _, thr, _, n_ge, _ = lax.while_loop(bisect_cond, bisect_body,
                                        (jnp.int32(0), lo0, hi0, c_lo0, done0))

    idx_bits = max(1, int(math.ceil(math.log2(seq))))

    def tie_limit():
        need = kf - count_hits(lambda blk, j: jnp.where(blk > thr, 1.0, 0.0))

        def step(i, lim):
            cand = lim | (jnp.int32(1) << (idx_bits - 1 - i))
            below = count_hits(lambda blk, j: jnp.where(
                blk == thr, jnp.where(j * rb + rrow < cand, 1.0, 0.0), 0.0))
            return jnp.where(below < need, cand, lim)
        return lax.fori_loop(0, idx_bits, step, jnp.zeros((1, tq), I32))

    has_excess = jnp.max(n_ge - kf) > 0.0
    lim = lax.cond(has_excess, tie_limit, lambda: jnp.full((1, tq), seq, I32))

    q2 = pair_stack(q_ref[...])

    def attention(plain):
        acc_ref[...] = jnp.zeros_like(acc_ref)

        def make_bias(c):
            off = pl.multiple_of(c * kc, kc)
            key = keys_ref[pl.ds(off, kc), :]
            if plain:
                bias_ref[...] = jnp.where(key >= thr, 0.0, NEG_BIG)
            else:
                rowi = off + row0
                tie_ok = jnp.where(rowi <= lim, 0.0, NEG_BIG)
                bias = jnp.where(key > thr, 0.0, jnp.where(key == thr, tie_ok, NEG_BIG))
                bias_ref[...] = jnp.where(rowi <= t_col, bias, NEG_BIG)

        def logits(c, g, base):
            off = pl.multiple_of(c * kc, kc)
            kg = k_ref[pl.ds(off, kc), g * LANES:(g + 1) * LANES]
            bias = bias_ref[...]
            s2 = _dot_nt(kg, q2[g]) + jnp.concatenate([bias, bias], axis=1)
            s_ref[base + g] = s2
            return jnp.max(s2, axis=0, keepdims=True)

        def consume(c, g, base, m_old, l_old, m_chunk):
            off = pl.multiple_of(c * kc, kc)
            vg = v_ref[pl.ds(off, kc), g * LANES:(g + 1) * LANES]
            m_new = jnp.maximum(m_old, m_chunk)
            alpha = jnp.exp2(m_old - m_new)
            p = jnp.exp2(s_ref[base + g] - m_new)
            l_new = alpha * l_old + jnp.sum(p, axis=0, keepdims=True)
            acc_ref[g] = alpha * acc_ref[g] + _dot_tn(vg, p.astype(BF16))
            return m_new, l_new

        def step(c, src, dst, state, stage_next):
            ms, ls, mcs = state
            if stage_next:
                make_bias(c + 1)
            new_m, new_l, new_mc = [], [], []
            for g in range(n_pairs):
                m_new, l_new = consume(c, g, src, ms[g], ls[g], mcs[g])
                new_m.append(m_new)
                new_l.append(l_new)
                if stage_next:
                    new_mc.append(logits(c + 1, g, dst))
            return tuple(new_m), tuple(new_l), tuple(new_mc) if stage_next else mcs

        def finish(state):
            ls = state[1]
            for g in range(n_pairs):
                o2 = acc_ref[g] / ls[g]
                o_ref[:, g * LANES:(g + 1) * LANES] = jnp.where(lane_first, o2[:, :tq].T,
                                                                o2[:, tq:].T)

        make_bias(0)
        state = (tuple(jnp.full((1, 2 * tq), NEG_BIG, F32) for _ in range(n_pairs)),
                 tuple(jnp.zeros((1, 2 * tq), F32) for _ in range(n_pairs)),
                 tuple(logits(0, g, 0) for g in range(n_pairs)))
        n_double = (n_kc - 1) // 2

        def double_step(i, state):
            state = step(2 * i, 0, n_pairs, state, True)
            return step(2 * i + 1, n_pairs, 0, state, True)

        state = lax.fori_loop(0, n_double, double_step, state)
        c_rest = 2 * n_double

        @pl.when(n_kc - c_rest == 1)
        def _():
            finish(step(c_rest, 0, n_pairs, state, False))

        @pl.when(n_kc - c_rest == 2)
        def _():
            st = step(c_rest, 0, n_pairs, state, True)
            finish(step(c_rest + 1, n_pairs, 0, st, False))

    plain = jnp.logical_and(jnp.logical_not(has_excess), qb * tq + 1 >= topk)
    pl.when(plain)(lambda: attention(True))
    pl.when(jnp.logical_not(plain))(lambda: attention(False))


def _dsa(qi, kw, q, ki, k, v, batch, seq):
    tq = DSA_TQ
    kc = DSA_KC if seq % DSA_KC == 0 else DSA_RB
    nqb = seq // tq
    topk = min(MAX_TOPK, seq // 4)
    qrow = lambda w: pl.BlockSpec((tq, w), lambda b, i: (b * nqb + i, 0))
    kvrow = lambda w: pl.BlockSpec((seq, w), lambda b, i: (b, 0))
    return pl.pallas_call(
        functools.partial(_dsa_kernel, seq=seq, topk=topk, kc=kc),
        grid=(batch, nqb),
        in_specs=[qrow(WIDTH), qrow(LANES), qrow(WIDTH), kvrow(LANES), kvrow(WIDTH), kvrow(WIDTH)],
        out_specs=qrow(WIDTH),
        out_shape=jax.ShapeDtypeStruct((batch * seq, WIDTH), F32),
        scratch_shapes=[pltpu.VMEM((seq, tq), F32),
                        pltpu.VMEM((WIDTH // LANES, LANES, 2 * tq), F32),
                        pltpu.VMEM((2 * (WIDTH // LANES), kc, 2 * tq), F32),
                        pltpu.VMEM((kc, tq), F32)],
        compiler_params=pltpu.CompilerParams(dimension_semantics=("arbitrary", "arbitrary"),
                                             vmem_limit_bytes=VMEM_LIMIT),
        name="dsa",
    )(qi, kw, q, ki, k, v)


def _tail_kernel(x_ref, ya_ref, yb_ref, gate_ref, wor_ref, woa_ref, wout_ref, g2_ref,
                 wfg_ref, wfu_ref, wfo_ref, gf_ref, o_ref):
    tm = x_ref.shape[0]
    rows = [slice(r * (tm // TAIL_SPLIT), (r + 1) * (tm // TAIL_SPLIT)) for r in range(TAIL_SPLIT)]
    half = D_FF // 2
    halves = [slice(f * half, (f + 1) * half) for f in range(2)]

    ma = [_dot(ya_ref[r, :].astype(BF16), wor_ref[...]) for r in rows]
    mb = [_dot(yb_ref[r, :].astype(BF16), woa_ref[...]) for r in rows]
    merged = [gate_ref[r, :D_MODEL] * a + gate_ref[r, D_MODEL:] * b for r, a, b in zip(rows, ma, mb)]
    h = [x_ref[r, :] + _dot(m.astype(BF16), wout_ref[...]) for r, m in zip(rows, merged)]
    hb = [(t * lax.rsqrt(jnp.mean(t * t, axis=-1, keepdims=True) + NORM_EPS)
           * g2_ref[...]).astype(BF16) for t in h]
    ffn = [jnp.zeros_like(t) for t in h]
    for sl in halves:
        zg = [_dot(t, wfg_ref[:, sl]) for t in hb]
        zu = [_dot(t, wfu_ref[:, sl]) for t in hb]
        act = [(g * _sigmoid(g) * u).astype(BF16) for g, u in zip(zg, zu)]
        ffn = [f + _dot(a, wfo_ref[sl, :]) for f, a in zip(ffn, act)]
    for r, t, f in zip(rows, h, ffn):
        t = t + f
        o_ref[r, :] = t * lax.rsqrt(jnp.mean(t * t, axis=-1, keepdims=True) + NORM_EPS) * gf_ref[...]


def _tail(x2, ya, yb, gate, w_o_rwkv, w_o_att, w_out, norm2_g, w_ffn_in, w_ffn_out, normf_g):
    tokens = x2.shape[0]
    tm = 512 if tokens % 512 == 0 else tokens
    row = lambda w: pl.BlockSpec((tm, w), lambda i: (i, 0))
    const = lambda shape: pl.BlockSpec(shape, lambda i: (0, 0), pipeline_mode=pl.Buffered(1))
    return pl.pallas_call(
        _tail_kernel,
        grid=(tokens // tm,),
        in_specs=[row(D_MODEL), row(WIDTH), row(WIDTH), row(GATE_COLS),
                  const((WIDTH, D_MODEL)), const((WIDTH, D_MODEL)), const((D_MODEL, D_MODEL)),
                  const((1, D_MODEL)), const((D_MODEL, D_FF)), const((D_MODEL, D_FF)),
                  const((D_FF, D_MODEL)), const((1, D_MODEL))],
        out_specs=row(D_MODEL),
        out_shape=jax.ShapeDtypeStruct((tokens, D_MODEL), F32),
        compiler_params=pltpu.CompilerParams(dimension_semantics=("arbitrary",),
                                             vmem_limit_bytes=VMEM_LIMIT),
        name="tail",
    )(x2, ya, yb, gate, w_o_rwkv.astype(BF16), w_o_att.astype(BF16), w_out.astype(BF16),
      norm2_g.reshape(1, -1), w_ffn_in[:, :D_FF].astype(BF16), w_ffn_in[:, D_FF:].astype(BF16),
      w_ffn_out.astype(BF16), normf_g.reshape(1, -1))


def kernel(x, norm1_g, w_in, tshift_mu, w_decay_up, w0, a_up, a0, g_up, k_k, k_a, r_k, lnx_g, lnx_b, w_o_rwkv, w_o_att, w_out, norm2_g, w_ffn_in, w_ffn_out, normf_g):
    batch, seq, _ = x.shape
    x2 = x.reshape(batch * seq, D_MODEL)
    ps, q, k, v, qi, ki, kw, gate = _project(x2, norm1_g, w_in, tshift_mu, seq)
    ya = _rwkv(ps, w_decay_up, w0, a_up, a0, g_up, k_k, k_a, r_k.reshape(-1), lnx_g, lnx_b,
               batch, seq)
    yb = _dsa(qi, kw, q, ki, k, v, batch, seq)
    out = _tail(x2, ya, yb, gate, w_o_rwkv, w_o_att, w_out, norm2_g, w_ffn_in, w_ffn_out, normf_g)
    return out.reshape(batch, seq, D_MODEL)
```

```python
import functools
import math

import jax
import jax.numpy as jnp
import numpy as np
from jax import lax
from jax.experimental import pallas as pl
from jax.experimental.pallas import tpu as pltpu

F32 = jnp.float32
BF16 = jnp.bfloat16
I32 = jnp.int32

D_MODEL = 1024
HEAD_DIM = 64
N_HEADS = 8
WIDTH = N_HEADS * HEAD_DIM
DECAY_RANK = 64
AAA_RANK = 64
GATE_RANK = 128
MAX_TOPK = 256
ROPE_THETA = 10000.0
D_FF = 2816
NORM_EPS = 1e-6
LNX_EPS = 64e-5
RWKV_COLS = 3 * WIDTH + DECAY_RANK + AAA_RANK + GATE_RANK
ATT_COLS = 3 * WIDTH
IDX_COLS = N_HEADS * HEAD_DIM + HEAD_DIM + N_HEADS
IDX_PAD = 640
GATE_COLS = 2 * D_MODEL
LANES = 128
VMEM_LIMIT = 56 * 1024 * 1024
TAIL_SPLIT = 2

INT_MIN = -(2 ** 31)
LOG2_E = 1.4426950408889634
NEG_BIG = -1e30


def _dot(a, b, precision=None):
    return jnp.dot(a, b, preferred_element_type=F32, precision=precision)


def _dot_nt(a, b, precision=None):
    return lax.dot_general(a, b, (((1,), (1,)), ((), ())), preferred_element_type=F32,
                           precision=precision)


def _dot_tn(a, b, precision=None):
    return lax.dot_general(a, b, (((0,), (0,)), ((), ())), preferred_element_type=F32,
                           precision=precision)


def _sigmoid(x):
    return 1.0 / (1.0 + jnp.exp(-x))


def _rope_group(xg, cos, sin_signed, lane_lo):
    partner = jnp.where(lane_lo, pltpu.roll(xg, LANES - HEAD_DIM // 2, 1),
                        pltpu.roll(xg, HEAD_DIM // 2, 1))
    return xg * cos + partner * sin_signed


def _proj_kernel(x_ref, g1_ref, wr_ref, wa_ref, wi_ref, wg_ref, mu_ref, cos_ref, sin_ref,
                 ps_ref, q_ref, k_ref, v_ref, qi_ref, ki_ref, kw_ref, gate_ref, carry_ref,
                 *, tiles_per_seq):
    i = pl.program_id(0)

    @pl.when(i % tiles_per_seq == 0)
    def _():
        carry_ref[...] = jnp.zeros_like(carry_ref)

    x = x_ref[...]
    tm = x.shape[0]
    u = x * lax.rsqrt(jnp.mean(x * x, axis=-1, keepdims=True) + NORM_EPS) * g1_ref[...]
    ub = u.astype(BF16)
    cos = cos_ref[...]
    sin_signed = sin_ref[...]
    lane = lax.broadcasted_iota(I32, (1, LANES), 1)
    lane_lo = (lane % HEAD_DIM) < HEAD_DIM // 2
    row = lax.broadcasted_iota(I32, (tm, 1), 0)

    def rwkv_epilogue(pr):
        prev = jnp.where(row == 0, carry_ref[7:8, :], pltpu.roll(pr, 1, 0))
        ps_ref[...] = pr + (prev - pr) * mu_ref[...]
        carry_ref[...] = pr[tm - 8:tm, :]

    def att_epilogue(pa):
        for g in range(WIDTH // LANES):
            sl = slice(g * LANES, (g + 1) * LANES)
            qg = _rope_group(pa[:, sl], cos, sin_signed, lane_lo) * (HEAD_DIM ** -0.5 * LOG2_E)
            q_ref[:, sl] = qg.astype(BF16)
            kg = _rope_group(pa[:, WIDTH + g * LANES:WIDTH + (g + 1) * LANES], cos, sin_signed,
                             lane_lo)
            k_ref[:, sl] = kg.astype(BF16)
        v_ref[...] = pa[:, 2 * WIDTH:].astype(BF16)

    def idx_epilogue(pi):
        for g in range(WIDTH // LANES):
            sl = slice(g * LANES, (g + 1) * LANES)
            qi_ref[:, sl] = _rope_group(pi[:, sl], cos, sin_signed, lane_lo).astype(BF16)
        tail = pi[:, WIDTH:WIDTH + LANES]
        kw_ref[...] = tail
        kr = _rope_group(tail, cos, sin_signed, lane_lo)
        ki_ref[...] = jnp.where(lane < HEAD_DIM, kr, pltpu.roll(kr, HEAD_DIM, 1)).astype(BF16)

    pr = _dot(ub, wr_ref[...])
    pa = _dot(ub, wa_ref[...])
    rwkv_epilogue(pr)
    pi = _dot(ub, wi_ref[...])
    att_epilogue(pa)
    pg = _dot(ub, wg_ref[...])
    idx_epilogue(pi)
    gate_ref[...] = _sigmoid(pg)


def _rope_tables(seq):
    half = HEAD_DIM // 2
    inv = 1.0 / (ROPE_THETA ** (jnp.arange(half, dtype=F32) * 2.0 / HEAD_DIM))
    ang = jnp.arange(seq, dtype=F32)[:, None] * inv[None, :]
    cos, sin = jnp.cos(ang), jnp.sin(ang)
    cos_t = jnp.concatenate([cos, cos, cos, cos], axis=1)
    sin_t = jnp.concatenate([-sin, sin, -sin, sin], axis=1)
    return cos_t, sin_t


def _full(shape):
    return pl.BlockSpec(shape, lambda *_: (0,) * len(shape))


def _project(x2, norm1_g, w_in, tshift_mu, seq):
    tokens = x2.shape[0]
    tm = 512 if seq % 512 == 0 else seq
    c0, c1, c2 = RWKV_COLS, RWKV_COLS + ATT_COLS, RWKV_COLS + ATT_COLS + IDX_COLS
    wr = w_in[:, :c0].astype(BF16)
    wa = w_in[:, c0:c1].astype(BF16)
    wi = jnp.pad(w_in[:, c1:c2], ((0, 0), (0, IDX_PAD - IDX_COLS))).astype(BF16)
    wg = w_in[:, c2:].astype(BF16)
    cos_t, sin_t = _rope_tables(seq)
    tps = seq // tm
    row = lambda w: pl.BlockSpec((tm, w), lambda i: (i, 0))
    pos = pl.BlockSpec((tm, LANES), lambda i: (i % tps, 0))
    out_shape = (
        jax.ShapeDtypeStruct((tokens, RWKV_COLS), F32),
        jax.ShapeDtypeStruct((tokens, WIDTH), BF16),
        jax.ShapeDtypeStruct((tokens, WIDTH), BF16),
        jax.ShapeDtypeStruct((tokens, WIDTH), BF16),
        jax.ShapeDtypeStruct((tokens, WIDTH), BF16),
        jax.ShapeDtypeStruct((tokens, LANES), BF16),
        jax.ShapeDtypeStruct((tokens, LANES), F32),
        jax.ShapeDtypeStruct((tokens, GATE_COLS), F32),
    )
    return pl.pallas_call(
        functools.partial(_proj_kernel, tiles_per_seq=tps),
        grid=(tokens // tm,),
        in_specs=[row(D_MODEL), _full((1, D_MODEL)), _full((D_MODEL, RWKV_COLS)),
                  _full((D_MODEL, ATT_COLS)), _full((D_MODEL, IDX_PAD)), _full((D_MODEL, GATE_COLS)),
                  _full((1, RWKV_COLS)), pos, pos],
        out_specs=(row(RWKV_COLS), row(WIDTH), row(WIDTH), row(WIDTH), row(WIDTH),
                   row(LANES), row(LANES), row(GATE_COLS)),
        out_shape=out_shape,
        scratch_shapes=[pltpu.VMEM((8, RWKV_COLS), F32)],
        compiler_params=pltpu.CompilerParams(dimension_semantics=("arbitrary",),
                                             vmem_limit_bytes=VMEM_LIMIT),
        name="proj",
    )(x2, norm1_g.reshape(1, -1), wr, wa, wi, wg, tshift_mu.reshape(1, -1), cos_t, sin_t)


RWKV_CHUNK = 64
RWKV_CHUNKS_PER_STEP = 4
HI = lax.Precision.HIGHEST


def _pair_sum(x, lane_first):
    s0 = jnp.sum(jnp.where(lane_first, x, 0.0), axis=1, keepdims=True)
    s1 = jnp.sum(jnp.where(lane_first, 0.0, x), axis=1, keepdims=True)
    return jnp.where(lane_first, s0, s1)


def _rwkv_kernel(ps_ref, wdu_ref, w0_ref, aup_ref, a0_ref, gup_ref, kk_ref, ka_ref, rk_ref,
                 lng_ref, lnb_ref, y_ref, state_ref):
    c = pl.program_id(1)
    L = RWKV_CHUNK

    @pl.when(c == 0)
    def _():
        state_ref[...] = jnp.zeros_like(state_ref)

    ps = ps_ref[...]
    r_all = ps[:, :WIDTH]
    k_all = ps[:, WIDTH:2 * WIDTH]
    v_all = ps[:, 2 * WIDTH:3 * WIDTH]
    o = 3 * WIDTH
    wd = ps[:, o:o + DECAY_RANK]
    ad = ps[:, o + DECAY_RANK:o + DECAY_RANK + AAA_RANK]
    gd = ps[:, o + DECAY_RANK + AAA_RANK:]

    bf = lambda t: t.astype(BF16)
    z = -(w0_ref[...] + _dot(bf(jnp.tanh(wd)), wdu_ref[...]))
    softplus = jnp.maximum(z, 0.0) + jnp.log(1.0 + jnp.exp(-jnp.abs(z)))
    w_all = -softplus - 0.5
    ld_all = -jnp.exp(w_all)
    a_all = _sigmoid(a0_ref[...] + _dot(bf(ad), aup_ref[...]))
    g_all = _dot(bf(_sigmoid(gd)), gup_ref[...])

    lane = lax.broadcasted_iota(I32, (1, LANES), 1)
    lane_first = lane < HEAD_DIM
    ti = lax.broadcasted_iota(I32, (L, 2 * L), 0)
    tj = lax.broadcasted_iota(I32, (L, 2 * L), 1) % L
    strict = ti > tj
    incl = ti >= tj
    bi = lax.broadcasted_iota(I32, (LANES, 2 * LANES), 0)
    bj = lax.broadcasted_iota(I32, (LANES, 2 * LANES), 1) % LANES
    same_head = (bi < HEAD_DIM) == (bj < HEAD_DIM)
    eye = (bi == bj)[:, :LANES]
    tri_ones = jnp.where(incl[:, :L], 1.0, 0.0)
    zeros_l = jnp.zeros((L, LANES), BF16)

    n_pairs = WIDTH // LANES
    lane2_first = jnp.concatenate([lane_first, lane_first], axis=1)
    pr = {}
    for ci in range(RWKV_CHUNKS_PER_STEP):
        rows = slice(ci * L, (ci + 1) * L)
        for g in range(n_pairs):
            sl = slice(g * LANES, (g + 1) * LANES)
            r, k, v = r_all[rows, sl], k_all[rows, sl], v_all[rows, sl]
            a, ld = a_all[rows, sl], ld_all[rows, sl]
            kk = k * kk_ref[:, sl]
            nrm = jnp.sqrt(_pair_sum(kk * kk, lane_first))
            kk = kk / jnp.maximum(nrm, 1e-12)
            k = k * (1.0 + (a - 1.0) * ka_ref[:, sl])
            cum = _dot(tri_ones, ld, HI)
            p_inc = jnp.exp(cum)
            p_inv = jnp.exp(-cum)
            al = -kk * jnp.exp(cum - ld)
            be = kk * a * p_inv
            kt = k * p_inv
            rt = r * p_inc
            p_last = p_inc[L - 1:L, :]
            v_b = bf(v)
            pr[ci, g] = dict(
                r=r, k=k, v=v, al=al, rt=rt, p_last=p_last,
                bk=bf(jnp.concatenate([be, kt], axis=0)),
                bkp=bf(jnp.concatenate([be * p_last, kt * p_last], axis=0)),
                zv=jnp.concatenate([zeros_l, v_b], axis=0),
                zv2=jnp.concatenate([zeros_l, v_b], axis=1))

    heads = [(ci, g, hh) for ci in range(RWKV_CHUNKS_PER_STEP) for g in range(n_pairs)
             for hh in range(2)]
    gram = []
    for ci, g, hh in heads:
        mask = lane_first if hh == 0 else jnp.logical_not(lane_first)
        ar = bf(jnp.concatenate([jnp.where(mask, pr[ci, g]["al"], 0.0),
                                 jnp.where(mask, pr[ci, g]["rt"], 0.0)], axis=0))
        gram.append(_dot_nt(ar, pr[ci, g]["bk"]))
    top = [jnp.where(strict, gm[:L], 0.0) for gm in gram]
    bot = [bf(jnp.where(incl, gm[L:], 0.0)) for gm in gram]
    rhs = [jnp.concatenate([pr[ci, g]["al"], _dot(bf(top[i]), pr[ci, g]["zv"])], axis=1)
           for i, (ci, g, hh) in enumerate(heads)]
    npow = [bf(t[:, :L]) for t in top]
    u = [rhs[i] + _dot(npow[i], bf(rhs[i])) for i in range(len(heads))]
    for _ in range(int(math.log2(L)) - 1):
        npow = [bf(_dot(n, n)) for n in npow]
        u = [u[i] + _dot(npow[i], bf(u[i])) for i in range(len(heads))]
    u_b = [bf(x) for x in u]
    o_h = [_dot(bot[i], jnp.concatenate([u_b[i], pr[ci, g]["zv2"]], axis=0))
           for i, (ci, g, hh) in enumerate(heads)]

    for g in range(n_pairs):
        sl = slice(g * LANES, (g + 1) * LANES)
        st = state_ref[g]
        for ci in range(RWKV_CHUNKS_PER_STEP):
            rows = slice(ci * L, (ci + 1) * L)
            p = pr[ci, g]
            i0 = (ci * n_pairs + g) * 2
            ub = jnp.where(lane2_first, u_b[i0], u_b[i0 + 1])
            o2 = jnp.where(lane2_first, o_h[i0], o_h[i0 + 1])
            ry = p["rt"] + o2[:, :LANES]
            yl = o2[:, LANES:]
            uz = jnp.concatenate([ub, p["zv2"]], axis=0)
            mg = jnp.where(same_head, _dot_tn(p["bkp"], uz), 0.0)
            m_mat = jnp.where(eye, p["p_last"], 0.0) + mg[:, :LANES]
            g_mat = mg[:, LANES:]

            st_b = bf(st)
            y = _dot(bf(ry), st_b) + yl
            st = _dot(bf(m_mat), st_b) + g_mat

            mean = _pair_sum(y, lane_first) * (1.0 / HEAD_DIM)
            yc = y - mean
            var = _pair_sum(yc * yc, lane_first) * (1.0 / HEAD_DIM)
            yn = yc * lax.rsqrt(var + LNX_EPS) * lng_ref[:, sl] + lnb_ref[:, sl]
            bonus = _pair_sum(p["r"] * p["k"] * rk_ref[:, sl], lane_first) * p["v"]
            y_ref[rows, sl] = ((yn + bonus) * g_all[rows, sl]).astype(BF16)
        state_ref[g] = st


def _rwkv(ps, w_decay_up, w0, a_up, a0, g_up, k_k, k_a, r_k, lnx_g, lnx_b, batch, seq):
    L = RWKV_CHUNK * RWKV_CHUNKS_PER_STEP
    n_chunks = seq // L
    vec = lambda t: t.reshape(1, WIDTH)
    return pl.pallas_call(
        _rwkv_kernel,
        grid=(batch, n_chunks),
        in_specs=[pl.BlockSpec((L, RWKV_COLS), lambda b, c: (b * n_chunks + c, 0)),
                  _full((DECAY_RANK, WIDTH)), _full((1, WIDTH)), _full((AAA_RANK, WIDTH)),
                  _full((1, WIDTH)), _full((GATE_RANK, WIDTH)), _full((1, WIDTH)),
                  _full((1, WIDTH)), _full((1, WIDTH)), _full((1, WIDTH)), _full((1, WIDTH))],
        out_specs=pl.BlockSpec((L, WIDTH), lambda b, c: (b * n_chunks + c, 0)),
        out_shape=jax.ShapeDtypeStruct((batch * seq, WIDTH), BF16),
        scratch_shapes=[pltpu.VMEM((WIDTH // LANES, LANES, LANES), F32)],
        compiler_params=pltpu.CompilerParams(dimension_semantics=("arbitrary", "arbitrary"),
                                             vmem_limit_bytes=VMEM_LIMIT),
        name="rwkv",
    )(ps, w_decay_up.astype(BF16), vec(w0), a_up.astype(BF16), vec(a0), g_up.astype(BF16),
      vec(k_k), vec(k_a), vec(r_k), vec(lnx_g), vec(lnx_b))


DSA_TQ = LANES
DSA_KC = 512
DSA_RB = 512
DSA_PASSES_PER_TEST = 4


def _dsa_kernel(qi_ref, kw_ref, q_ref, ki_ref, k_ref, v_ref, o_ref, keys_ref, acc_ref,
                s_ref, bias_ref, *, seq, topk, kc):
    tq = DSA_TQ
    qb = pl.program_id(1)
    n_kc = (qb * tq + tq + kc - 1) // kc
    t_col = qb * tq + lax.broadcasted_iota(I32, (1, tq), 1)
    row0 = lax.broadcasted_iota(I32, (kc, 1), 0)
    lane = lax.broadcasted_iota(I32, (1, LANES), 1)
    lane_first = lane < HEAD_DIM
    idx_scale = (HEAD_DIM ** -0.5) * (N_HEADS ** -0.5)
    zero = jnp.zeros((), BF16)
    n_pairs = WIDTH // LANES

    def pair_stack(x):
        out = []
        for g in range(WIDTH // LANES):
            xg = x[:, g * LANES:(g + 1) * LANES]
            out.append(jnp.concatenate([jnp.where(lane_first, xg, zero),
                                        jnp.where(lane_first, zero, xg)], axis=0))
        return out

    qi2 = pair_stack(qi_ref[...])
    kw_t = kw_ref[...].T
    w_h = [kw_t[HEAD_DIM + h:HEAD_DIM + h + 1, :] * idx_scale for h in range(N_HEADS)]

    def stage_scores(c, g, base):
        off = pl.multiple_of(c * kc, kc)
        s_ref[base + g] = _dot_nt(ki_ref[pl.ds(off, kc), :], qi2[g])

    def fold_scores(g, base, acc):
        s2 = s_ref[base + g]
        acc = acc + jnp.maximum(s2[:, :tq], 0.0) * w_h[2 * g]
        return acc + jnp.maximum(s2[:, tq:], 0.0) * w_h[2 * g + 1]

    def store_keys(c, acc):
        off = pl.multiple_of(c * kc, kc)
        score = jnp.where(acc == 0.0, 0.0, acc)
        keys_ref[pl.ds(off, kc), :] = jnp.where(off + row0 <= t_col, score, -jnp.inf)

    def score_step(c, src, dst, stage_next):
        acc = jnp.zeros((kc, tq), F32)
        for g in range(n_pairs):
            acc = fold_scores(g, src, acc)
            if stage_next:
                stage_scores(c + 1, g, dst)
        store_keys(c, acc)

    for g in range(n_pairs):
        stage_scores(0, g, 0)
    n_double = (n_kc - 1) // 2

    def score_double_step(i, carry):
        score_step(2 * i, 0, n_pairs, True)
        score_step(2 * i + 1, n_pairs, 0, True)
        return carry

    lax.fori_loop(0, n_double, score_double_step, 0)
    c_rest = 2 * n_double

    @pl.when(n_kc - c_rest == 1)
    def _():
        score_step(c_rest, 0, n_pairs, False)

    @pl.when(n_kc - c_rest == 2)
    def _():
        score_step(c_rest, 0, n_pairs, True)
        score_step(c_rest + 1, n_pairs, 0, False)

    rb = DSA_RB
    n_blk = n_kc * (kc // rb)
    rrow = lax.broadcasted_iota(I32, (rb, 1), 0)

    def scan(fn, init):
        def body(j, carry):
            return fn(keys_ref[pl.ds(pl.multiple_of(j * rb, rb), rb), :], j, carry)
        return lax.fori_loop(0, n_blk, body, init)

    def tree(x, op):
        parts = [x[r * 8:(r + 1) * 8, :] for r in range(rb // 8)]
        while len(parts) > 1:
            parts = [op(parts[i], parts[i + 1]) for i in range(0, len(parts), 2)]
        return parts[0]

    def count_hits(hits):
        cnt = scan(lambda blk, j, cnt: cnt + tree(hits(blk, j), jnp.add), jnp.zeros((8, tq), F32))
        return jnp.sum(cnt, axis=0, keepdims=True)

    def count_ge(cand):
        return count_hits(lambda blk, j: jnp.where(blk >= cand, 1.0, 0.0))

    kf = float(topk)
    tiny = float(np.finfo(np.float32).tiny)
    n_valid = (t_col + 1).astype(F32)

    def first_pass(blk, j, carry):
        cnt, mx = carry
        return (cnt + tree(jnp.where(blk >= 0.0, 1.0, 0.0), jnp.add), jnp.maximum(mx, tree(blk, jnp.maximum)))

    cnt0, mx = scan(first_pass, (jnp.zeros((8, tq), F32), jnp.full((8, tq), -jnp.inf, F32)))
    n_pos = jnp.sum(cnt0, axis=0, keepdims=True)
    top = jnp.max(mx, axis=0, keepdims=True)
    tbits = pltpu.bitcast(top, I32)
    above = pltpu.bitcast(jnp.where(top > 0.0, tbits + 1, tbits - 1), F32)
    above = jnp.where(top == 0.0, tiny, above)
    nonneg = n_pos >= kf

    def lowest():
        def body(blk, j, mn):
            return jnp.minimum(mn, tree(jnp.where(blk == -jnp.inf, jnp.inf, blk), jnp.minimum))
        mn = scan(body, jnp.full((8, tq), jnp.inf, F32))
        return jnp.min(mn, axis=0, keepdims=True)

    need_low = jnp.max(jnp.where(jnp.logical_and(jnp.logical_not(nonneg), n_valid > kf), 1.0, 0.0)) > 0.0
    low = lax.cond(need_low, lowest, lambda: jnp.zeros((1, tq), F32))

    short = n_valid <= kf
    lo0 = jnp.where(short, -jnp.inf, jnp.where(nonneg, 0.0, low))
    hi0 = jnp.where(nonneg, above, 0.0)
    c_lo0 = jnp.where(short, n_valid, jnp.where(nonneg, n_pos, n_valid))
    done0 = jnp.where(jnp.logical_or(short, c_lo0 == kf), 1.0, 0.0)

    def bisect_cond(st):
        i, lo, hi, c_lo, done = st
        return jnp.min(done) < 0.5

    def bisect_body(st):
        i, lo, hi, c_lo, done = st
        for _ in range(DSA_PASSES_PER_TEST):
            mid = lo * 0.5 + hi * 0.5
            mid = jnp.where(i == 0, jnp.where(lo == 0.0, tiny, jnp.where(hi == 0.0, -tiny, mid)), mid)
            stuck = jnp.logical_or(mid <= lo, mid >= hi)
            n = count_ge(mid)
            live = jnp.logical_and(done < 0.5, jnp.logical_not(stuck))
            up = jnp.logical_and(live, n >= kf)
            lo = jnp.where(up, mid, lo)
            c_lo = jnp.where(up, n, c_lo)
            hi = jnp.where(jnp.logical_and(live, n < kf), mid, hi)
            done = jnp.where(jnp.logical_or(stuck, jnp.logical_and(live, n == kf)), 1.0, done)
            i = i + 1
        return i, lo, hi, c_lo, done

    _, thr, _, n_ge, _ = lax.while_loop(bisect_cond, bisect_body,
                                        (jnp.int32(0), lo0, hi0, c_lo0, done0))

    idx_bits = max(1, int(math.ceil(math.log2(seq))))

    def tie_limit():
        need = kf - count_hits(lambda blk, j: jnp.where(blk > thr, 1.0, 0.0))

        def step(i, lim):
            cand = lim | (jnp.int32(1) << (idx_bits - 1 - i))
            below = count_hits(lambda blk, j: jnp.where(
                blk == thr, jnp.where(j * rb + rrow < cand, 1.0, 0.0), 0.0))
            return jnp.where(below < need, cand, lim)
        return lax.fori_loop(0, idx_bits, step, jnp.zeros((1, tq), I32))

    has_excess = jnp.max(n_ge - kf) > 0.0
    lim = lax.cond(has_excess, tie_limit, lambda: jnp.full((1, tq), seq, I32))

    q2 = pair_stack(q_ref[...])

    def attention(plain):
        acc_ref[...] = jnp.zeros_like(acc_ref)

        def make_bias(c):
            off = pl.multiple_of(c * kc, kc)
            key = keys_ref[pl.ds(off, kc), :]
            if plain:
                bias_ref[...] = jnp.where(key >= thr, 0.0, NEG_BIG)
            else:
                rowi = off + row0
                tie_ok = jnp.where(rowi <= lim, 0.0, NEG_BIG)
                bias = jnp.where(key > thr, 0.0, jnp.where(key == thr, tie_ok, NEG_BIG))
                bias_ref[...] = jnp.where(rowi <= t_col, bias, NEG_BIG)

        def logits(c, g, base):
            off = pl.multiple_of(c * kc, kc)
            kg = k_ref[pl.ds(off, kc), g * LANES:(g + 1) * LANES]
            bias = bias_ref[...]
            s2 = _dot_nt(kg, q2[g]) + jnp.concatenate([bias, bias], axis=1)
            s_ref[base + g] = s2
            return jnp.max(s2, axis=0, keepdims=True)

        def consume(c, g, base, m_old, l_old, m_chunk):
            off = pl.multiple_of(c * kc, kc)
            vg = v_ref[pl.ds(off, kc), g * LANES:(g + 1) * LANES]
            m_new = jnp.maximum(m_old, m_chunk)
            alpha = jnp.exp2(m_old - m_new)
            p = jnp.exp2(s_ref[base + g] - m_new)
            l_new = alpha * l_old + jnp.sum(p, axis=0, keepdims=True)
            acc_ref[g] = alpha * acc_ref[g] + _dot_tn(vg, p.astype(BF16))
            return m_new, l_new

        def step(c, src, dst, state, stage_next):
            ms, ls, mcs = state
            if stage_next:
                make_bias(c + 1)
            new_m, new_l, new_mc = [], [], []
            for g in range(n_pairs):
                m_new, l_new = consume(c, g, src, ms[g], ls[g], mcs[g])
                new_m.append(m_new)
                new_l.append(l_new)
                if stage_next:
                    new_mc.append(logits(c + 1, g, dst))
            return tuple(new_m), tuple(new_l), tuple(new_mc) if stage_next else mcs

        def finish(state):
            ls = state[1]
            for g in range(n_pairs):
                o2 = acc_ref[g] / ls[g]
                o_ref[:, g * LANES:(g + 1) * LANES] = jnp.where(
                    lane_first, o2[:, :tq].T, o2[:, tq:].T).astype(BF16)

        make_bias(0)
        state = (tuple(jnp.full((1, 2 * tq), NEG_BIG, F32) for _ in range(n_pairs)),
                 tuple(jnp.zeros((1, 2 * tq), F32) for _ in range(n_pairs)),
                 tuple(logits(0, g, 0) for g in range(n_pairs)))
        n_double = (n_kc - 1) // 2

        def double_step(i, state):
            state = step(2 * i, 0, n_pairs, state, True)
            return step(2 * i + 1, n_pairs, 0, state, True)

        state = lax.fori_loop(0, n_double, double_step, state)
        c_rest = 2 * n_double

        @pl.when(n_kc - c_rest == 1)
        def _():
            finish(step(c_rest, 0, n_pairs, state, False))

        @pl.when(n_kc - c_rest == 2)
        def _():
            st = step(c_rest, 0, n_pairs, state, True)
            finish(step(c_rest + 1, n_pairs, 0, st, False))

    plain = jnp.logical_and(jnp.logical_not(has_excess), qb * tq + 1 >= topk)
    pl.when(plain)(lambda: attention(True))
    pl.when(jnp.logical_not(plain))(lambda: attention(False))


def _dsa(qi, kw, q, ki, k, v, batch, seq):
    tq = DSA_TQ
    kc = DSA_KC if seq % DSA_KC == 0 else DSA_RB
    nqb = seq // tq
    topk = min(MAX_TOPK, seq // 4)
    qrow = lambda w: pl.BlockSpec((tq, w), lambda b, i: (b * nqb + i, 0))
    kvrow = lambda w: pl.BlockSpec((seq, w), lambda b, i: (b, 0))
    return pl.pallas_call(
        functools.partial(_dsa_kernel, seq=seq, topk=topk, kc=kc),
        grid=(batch, nqb),
        in_specs=[qrow(WIDTH), qrow(LANES), qrow(WIDTH), kvrow(LANES), kvrow(WIDTH), kvrow(WIDTH)],
        out_specs=qrow(WIDTH),
        out_shape=jax.ShapeDtypeStruct((batch * seq, WIDTH), BF16),
        scratch_shapes=[pltpu.VMEM((seq, tq), F32),
                        pltpu.VMEM((WIDTH // LANES, LANES, 2 * tq), F32),
                        pltpu.VMEM((2 * (WIDTH // LANES), kc, 2 * tq), F32),
                        pltpu.VMEM((kc, tq), F32)],
        compiler_params=pltpu.CompilerParams(dimension_semantics=("arbitrary", "arbitrary"),
                                             vmem_limit_bytes=VMEM_LIMIT),
        name="dsa",
    )(qi, kw, q, ki, k, v)


def _tail_kernel(x_ref, ya_ref, yb_ref, gate_ref, wor_ref, woa_ref, wout_ref, g2_ref,
                 wfg_ref, wfu_ref, wfo_ref, gf_ref, o_ref):
    tm = x_ref.shape[0]
    rows = [slice(r * (tm // TAIL_SPLIT), (r + 1) * (tm // TAIL_SPLIT)) for r in range(TAIL_SPLIT)]
    half = D_FF // 2
    halves = [slice(f * half, (f + 1) * half) for f in range(2)]

    ma = [_dot(ya_ref[r, :].astype(BF16), wor_ref[...]) for r in rows]
    mb = [_dot(yb_ref[r, :].astype(BF16), woa_ref[...]) for r in rows]
    merged = [gate_ref[r, :D_MODEL] * a + gate_ref[r, D_MODEL:] * b for r, a, b in zip(rows, ma, mb)]
    h = [x_ref[r, :] + _dot(m.astype(BF16), wout_ref[...]) for r, m in zip(rows, merged)]
    hb = [(t * lax.rsqrt(jnp.mean(t * t, axis=-1, keepdims=True) + NORM_EPS)
           * g2_ref[...]).astype(BF16) for t in h]
    ffn = [jnp.zeros_like(t) for t in h]
    for sl in halves:
        zg = [_dot(t, wfg_ref[:, sl]) for t in hb]
        zu = [_dot(t, wfu_ref[:, sl]) for t in hb]
        act = [(g * _sigmoid(g) * u).astype(BF16) for g, u in zip(zg, zu)]
        ffn = [f + _dot(a, wfo_ref[sl, :]) for f, a in zip(ffn, act)]
    for r, t, f in zip(rows, h, ffn):
        t = t + f
        o_ref[r, :] = t * lax.rsqrt(jnp.mean(t * t, axis=-1, keepdims=True) + NORM_EPS) * gf_ref[...]


def _tail(x2, ya, yb, gate, w_o_rwkv, w_o_att, w_out, norm2_g, w_ffn_in, w_ffn_out, normf_g):
    tokens = x2.shape[0]
    tm = 512 if tokens % 512 == 0 else tokens
    row = lambda w: pl.BlockSpec((tm, w), lambda i: (i, 0))
    const = lambda shape: pl.BlockSpec(shape, lambda i: (0, 0), pipeline_mode=pl.Buffered(1))
    return pl.pallas_call(
        _tail_kernel,
        grid=(tokens // tm,),
        in_specs=[row(D_MODEL), row(WIDTH), row(WIDTH), row(GATE_COLS),
                  const((WIDTH, D_MODEL)), const((WIDTH, D_MODEL)), const((D_MODEL, D_MODEL)),
                  const((1, D_MODEL)), const((D_MODEL, D_FF)), const((D_MODEL, D_FF)),
                  const((D_FF, D_MODEL)), const((1, D_MODEL))],
        out_specs=row(D_MODEL),
        out_shape=jax.ShapeDtypeStruct((tokens, D_MODEL), F32),
        compiler_params=pltpu.CompilerParams(dimension_semantics=("arbitrary",),
                                             vmem_limit_bytes=VMEM_LIMIT),
        name="tail",
    )(x2, ya, yb, gate, w_o_rwkv.astype(BF16), w_o_att.astype(BF16), w_out.astype(BF16),
      norm2_g.reshape(1, -1), w_ffn_in[:, :D_FF].astype(BF16), w_ffn_in[:, D_FF:].astype(BF16),
      w_ffn_out.astype(BF16), normf_g.reshape(1, -1))


def kernel(x, norm1_g, w_in, tshift_mu, w_decay_up, w0, a_up, a0, g_up, k_k, k_a, r_k, lnx_g, lnx_b, w_o_rwkv, w_o_att, w_out, norm2_g, w_ffn_in, w_ffn_out, normf_g):
    batch, seq, _ = x.shape
    x2 = x.reshape(batch * seq, D_MODEL)
    ps, q, k, v, qi, ki, kw, gate = _project(x2, norm1_g, w_in, tshift_mu, seq)
    ya = _rwkv(ps, w_decay_up, w0, a_up, a0, g_up, k_k, k_a, r_k.reshape(-1), lnx_g, lnx_b,
               batch, seq)
    yb = _dsa(qi, kw, q, ki, k, v, batch, seq)
    out = _tail(x2, ya, yb, gate, w_o_rwkv, w_o_att, w_out, norm2_g, w_ffn_in, w_ffn_out, normf_g)
    return out.reshape(batch, seq, D_MODEL)
```
